```python
import math
import jax, jax.numpy as jnp
from jax import lax
import numpy as np

D_MODEL = 4096
BATCH = 4
SEQ = 4096
DEPTH = 2
DEC_BATCH = 32
DEC_SEQ = 64
PAST_LEN = 4096

CHUNK = 64
EPS = 1e-6
GDN_DK = 128
GDN_DV = 128
GDN_WIDTH = 3 * D_MODEL // 8
GDN_HEADS = GDN_WIDTH // GDN_DV
GDN_QK = GDN_HEADS * GDN_DK
GDN_CONV_DIM = 2 * GDN_QK + GDN_WIDTH
CONV_W = 4
RET_DK = 128
RET_DV = 256
RET_WIDTH = 3 * D_MODEL // 8
RET_HEADS = RET_WIDTH // RET_DV
RET_QK = RET_HEADS * RET_DK
ROPE_BASE = 10000.0
POOL_WINDOWS = (2, 4, 8, 16)
POOL_WIDTH = D_MODEL - GDN_WIDTH - RET_WIDTH
POOL_GC = POOL_WIDTH // len(POOL_WINDOWS)
POOL_BUF = max(POOL_WINDOWS) - 1
MIX_WIDTH = GDN_WIDTH + RET_WIDTH + POOL_WIDTH
IN_SIZES = (GDN_CONV_DIM, GDN_WIDTH, GDN_HEADS, GDN_HEADS, RET_QK, RET_QK, RET_WIDTH, RET_WIDTH, POOL_WIDTH)
IN_DIM = sum(IN_SIZES)
D_FF = ((8 * D_MODEL // 3 + 255) // 256) * 256

kernel_name = 'hybrid_gdn_retention_pool_stream_step'


def _rms_norm(x, g):
    xf = x.astype(jnp.float32)
    y = xf * lax.rsqrt(jnp.mean(xf * xf, axis=-1, keepdims=True) + EPS)
    return (y * g.astype(jnp.float32)).astype(x.dtype)


def _swiglu(x, w1, w3, w2):
    return (jax.nn.silu(x @ w1) * (x @ w3)) @ w2


def _split_cols(t, sizes):
    out, start = [], 0
    for s in sizes:
        out.append(t[..., start:start + s])
        start += s
    return out


def _heads(t, n, d):
    b, l, _ = t.shape
    return t.reshape(b, l, n, d).transpose(0, 2, 1, 3)


def _merge(t):
    b, h, l, d = t.shape
    return t.transpose(0, 2, 1, 3).reshape(b, l, h * d)


def _l2norm(t):
    return t * lax.rsqrt(jnp.sum(t * t, axis=-1, keepdims=True) + EPS)


def _rope(t, pos):
    half = t.shape[-1] // 2
    inv_freq = ROPE_BASE ** (-jnp.arange(half, dtype=jnp.float32) / half)
    ang = pos.astype(jnp.float32)[:, None] * inv_freq[None, :]
    cos, sin = jnp.cos(ang), jnp.sin(ang)
    t1, t2 = t[..., :half], t[..., half:]
    return jnp.concatenate([t1 * cos - t2 * sin, t1 * sin + t2 * cos], axis=-1)


def _gated_delta_chunked(q, k, v, g, beta, s0):
    b, h, l, dk = q.shape
    dv = v.shape[-1]
    c = min(CHUNK, l)
    n = l // c
    blk = lambda t: t.reshape(b, h, n, c, *t.shape[3:])
    q, k, v, g, beta = blk(q), blk(k), blk(v), blk(g), blk(beta)
    G = jnp.cumsum(g, axis=-1)
    idx = jnp.arange(c)
    causal = idx[:, None] >= idx[None, :]
    strict = idx[:, None] > idx[None, :]
    gdiff = G[..., :, None] - G[..., None, :]
    decay = jnp.where(causal, jnp.exp(jnp.where(causal, gdiff, 0.0)), 0.0)
    kk = jnp.einsum('bhnid,bhnjd->bhnij', k, k)
    lower = jnp.where(strict, kk * decay * beta[..., :, None], 0.0)
    tri = lower + jnp.eye(c, dtype=jnp.float32)
    rhs = jnp.concatenate([v * beta[..., None], k * (beta * jnp.exp(G))[..., None]], axis=-1)
    sol = lax.linalg.triangular_solve(tri, rhs, left_side=True, lower=True, unit_diagonal=True)
    u_base, w_dec = sol[..., :dv], sol[..., dv:]
    qk = jnp.einsum('bhnid,bhnjd->bhnij', q, k) * decay
    q_dec = q * jnp.exp(G)[..., None]
    k_tail = k * jnp.exp(G[..., -1:] - G)[..., None]
    g_blk = jnp.exp(G[..., -1])

    def step(s, xs):
        u_b, w_d, qk_n, q_n, k_n, g_n = xs
        u = u_b - jnp.einsum('bhik,bhkv->bhiv', w_d, s)
        o = jnp.einsum('bhik,bhkv->bhiv', q_n, s) + jnp.einsum('bhij,bhjv->bhiv', qk_n, u)
        s = s * g_n[..., None, None] + jnp.einsum('bhik,bhiv->bhkv', k_n, u)
        return s, o

    xs = tuple(jnp.moveaxis(t, 2, 0) for t in (u_base, w_dec, qk, q_dec, k_tail, g_blk))
    s, o = lax.scan(step, s0, xs)
    return jnp.moveaxis(o, 0, 2).reshape(b, h, l, dv), s


def _gdn_mixer(qkv, z, a, bl, conv_buf, s0, conv_w, a_log, dt_bias, norm_w):
    f32 = jnp.float32
    bsz, l, _ = qkv.shape
    xp = jnp.concatenate([conv_buf.astype(qkv.dtype), qkv], axis=1)
    xpf = xp.astype(f32)
    cw = conv_w.astype(f32)
    conv = xpf[:, 0:l] * cw[0]
    for i in range(1, CONV_W):
        conv = conv + xpf[:, i:i + l] * cw[i]
    conv = jax.nn.silu(conv)
    new_buf = xp[:, l:]
    q, k, v = _split_cols(conv, (GDN_QK, GDN_QK, GDN_WIDTH))
    q = _l2norm(_heads(q, GDN_HEADS, GDN_DK)) * GDN_DK ** -0.5
    k = _l2norm(_heads(k, GDN_HEADS, GDN_DK))
    v = _heads(v, GDN_HEADS, GDN_DV)
    g = -jnp.exp(a_log.astype(f32)) * jax.nn.softplus(a.astype(f32) + dt_bias.astype(f32))
    beta = jax.nn.sigmoid(bl.astype(f32))
    o, s = _gated_delta_chunked(q, k, v, g.transpose(0, 2, 1), beta.transpose(0, 2, 1), s0.astype(f32))
    o = o.transpose(0, 2, 1, 3)
    o = o * lax.rsqrt(jnp.mean(o * o, axis=-1, keepdims=True) + EPS) * norm_w.astype(f32)
    o = o * jax.nn.silu(z.astype(f32).reshape(bsz, l, GDN_HEADS, GDN_DV))
    return o.reshape(bsz, l, GDN_WIDTH).astype(qkv.dtype), new_buf, s.astype(s0.dtype)


def _retention_chunked(q, k, v, s0):
    b, h, l, dk = q.shape
    dv = v.shape[-1]
    c = min(CHUNK, l)
    n = l // c
    lg = jnp.log1p(-jnp.exp2(-5.0 - jnp.arange(h, dtype=jnp.float32)))
    idx = jnp.arange(c, dtype=jnp.float32)
    rel = idx[:, None] - idx[None, :]
    causal = rel >= 0
    dmat = jnp.where(causal, jnp.exp(jnp.where(causal, rel, 0.0) * lg[:, None, None]), 0.0)
    qc = q.reshape(b, h, n, c, dk)
    kc = k.reshape(b, h, n, c, dk)
    vc = v.reshape(b, h, n, c, dv)
    scores = jnp.einsum('bhnik,bhnjk->bhnij', qc, kc) * dmat[None, :, None]
    inner = jnp.einsum('bhnij,bhnjv->bhniv', scores, vc)
    q_dec = qc * jnp.exp((idx + 1.0)[None, :] * lg[:, None])[None, :, None, :, None]
    k_tail = kc * jnp.exp((c - 1.0 - idx)[None, :] * lg[:, None])[None, :, None, :, None]
    g_blk = jnp.exp(c * lg)[None, :, None, None]

    def step(s, xs):
        inner_n, q_n, k_n, v_n = xs
        o = inner_n + jnp.einsum('bhik,bhkv->bhiv', q_n, s)
        s = s * g_blk + jnp.einsum('bhik,bhiv->bhkv', k_n, v_n)
        return s, o

    xs = tuple(jnp.moveaxis(t, 2, 0) for t in (inner, q_dec, k_tail, vc))
    s, o = lax.scan(step, s0, xs)
    return jnp.moveaxis(o, 0, 2).reshape(b, h, l, dv), s


def _retention_mixer(q, k, v, gate, pos, s0):
    f32 = jnp.float32
    q = _rope(_heads(q.astype(f32), RET_HEADS, RET_DK), pos)
    k = _rope(_heads(k.astype(f32), RET_HEADS, RET_DK), pos) * RET_DK ** -0.5
    v = _heads(v.astype(f32), RET_HEADS, RET_DV)
    o, s = _retention_chunked(q, k, v, s0.astype(f32))
    mu = jnp.mean(o, axis=-1, keepdims=True)
    oc = o - mu
    o = oc * lax.rsqrt(jnp.mean(oc * oc, axis=-1, keepdims=True) + EPS)
    o = _merge(o) * jax.nn.silu(gate.astype(f32))
    return o.astype(gate.dtype), s.astype(s0.dtype)


def _pool_mixer(u, buf, pos, w_grp, scale):
    f32 = jnp.float32
    b, l, _ = u.shape
    up = jnp.concatenate([buf.astype(u.dtype), u], axis=1)
    upf = up.astype(f32)
    cs = jnp.concatenate([jnp.zeros((b, 1, POOL_WIDTH), f32), jnp.cumsum(upf, axis=1)], axis=1)
    means = []
    for gi, w in enumerate(POOL_WINDOWS):
        sl = slice(gi * POOL_GC, (gi + 1) * POOL_GC)
        win = cs[:, POOL_BUF + 1:, sl] - cs[:, POOL_BUF + 1 - w:POOL_BUF + 1 - w + l, sl]
        cnt = jnp.minimum(pos + 1, w).astype(f32)
        means.append(win / cnt[None, :, None])
    d = jnp.concatenate(means, axis=-1) - upf[:, POOL_BUF:]
    d = d.reshape(b, l, len(POOL_WINDOWS), POOL_GC)
    y = jnp.einsum('blgc,gcd->blgd', d, w_grp.astype(f32)) * scale.astype(f32).reshape(len(POOL_WINDOWS), POOL_GC)
    return y.reshape(b, l, POOL_WIDTH).astype(u.dtype), up[:, l:]


def _trunk(x, pos0, conv_buf, gdn_s, ret_s, pool_buf, ffn1_norm, ffn1_w1, ffn1_w3, ffn1_w2, mix_norm, w_in,
           conv_w, gdn_a_log, gdn_dt_bias, gdn_norm_w, pool_w, pool_scale, w_out, ffn2_norm, ffn2_w1,
           ffn2_w3, ffn2_w2, final_norm):
    l = x.shape[1]
    pos = pos0 + jnp.arange(l, dtype=jnp.int32)
    new_conv, new_gdn, new_ret, new_pool = [], [], [], []
    for i in range(DEPTH):
        h = _rms_norm(x, ffn1_norm[i])
        x = x + 0.5 * _swiglu(h, ffn1_w1[i], ffn1_w3[i], ffn1_w2[i])
        h = _rms_norm(x, mix_norm[i])
        p = h @ w_in[i]
        qkv_a, z_a, a_a, b_a, q_b, k_b, v_b, g_b, u_c = _split_cols(p, IN_SIZES)
        o_a, cb, sa = _gdn_mixer(qkv_a, z_a, a_a, b_a, conv_buf[i], gdn_s[i], conv_w[i], gdn_a_log[i],
                                 gdn_dt_bias[i], gdn_norm_w[i])
        o_b, sb = _retention_mixer(q_b, k_b, v_b, g_b, pos, ret_s[i])
        o_c, pb = _pool_mixer(u_c, pool_buf[i], pos, pool_w[i], pool_scale[i])
        x = x + jnp.concatenate([o_a, o_b, o_c], axis=-1) @ w_out[i]
        h = _rms_norm(x, ffn2_norm[i])
        x = x + 0.5 * _swiglu(h, ffn2_w1[i], ffn2_w3[i], ffn2_w2[i])
        new_conv.append(cb)
        new_gdn.append(sa)
        new_ret.append(sb)
        new_pool.append(pb)
    y = _rms_norm(x, final_norm)
    return y, jnp.stack(new_conv), jnp.stack(new_gdn), jnp.stack(new_ret), jnp.stack(new_pool)


def setup_inputs(seed: int = 0) -> dict:
    key = jax.random.key(seed)
    ks = jax.random.split(key, 24)
    f32 = jnp.float32

    def nrm(k, shape, scale):
        return jax.random.normal(k, shape, f32) * scale

    def gain(k, shape):
        return 1.0 + 0.02 * jax.random.normal(k, shape, f32)

    a_log = jnp.log(jax.random.uniform(ks[10], (DEPTH, GDN_HEADS), f32, 1.0, 16.0))
    dt = jnp.exp(jax.random.uniform(ks[11], (DEPTH, GDN_HEADS), f32, math.log(1e-3), math.log(1e-1)))
    dt_bias = dt + jnp.log(-jnp.expm1(-dt))
    return {
        'x_prompt': nrm(ks[0], (BATCH, SEQ, D_MODEL), 1.0),
        'x_sample': nrm(ks[1], (DEC_BATCH, DEC_SEQ, D_MODEL), 1.0),
        'state_conv': nrm(ks[2], (DEPTH, DEC_BATCH, CONV_W - 1, GDN_CONV_DIM), 1.0),
        'state_gdn': nrm(ks[3], (DEPTH, DEC_BATCH, GDN_HEADS, GDN_DK, GDN_DV), 0.1),
        'state_ret': nrm(ks[4], (DEPTH, DEC_BATCH, RET_HEADS, RET_DK, RET_DV), 0.5),
        'state_pool': nrm(ks[5], (DEPTH, DEC_BATCH, POOL_BUF, POOL_WIDTH), 1.0),
        'ffn1_norm': gain(ks[6], (DEPTH, D_MODEL)),
        'ffn1_w1': nrm(ks[7], (DEPTH, D_MODEL, D_FF), D_MODEL ** -0.5),
        'ffn1_w3': nrm(ks[8], (DEPTH, D_MODEL, D_FF), D_MODEL ** -0.5),
        'ffn1_w2': nrm(ks[9], (DEPTH, D_FF, D_MODEL), D_FF ** -0.5),
        'mix_norm': gain(ks[12], (DEPTH, D_MODEL)),
        'w_in': nrm(ks[13], (DEPTH, D_MODEL, IN_DIM), D_MODEL ** -0.5),
        'conv_w': nrm(ks[14], (DEPTH, CONV_W, GDN_CONV_DIM), CONV_W ** -0.5),
        'gdn_a_log': a_log,
        'gdn_dt_bias': dt_bias,
        'gdn_norm_w': gain(ks[15], (DEPTH, GDN_DV)),
        'pool_w': nrm(ks[16], (DEPTH, len(POOL_WINDOWS), POOL_GC, POOL_GC), POOL_GC ** -0.5),
        'pool_scale': gain(ks[17], (DEPTH, POOL_WIDTH)),
        'w_out': nrm(ks[18], (DEPTH, MIX_WIDTH, D_MODEL), MIX_WIDTH ** -0.5),
        'ffn2_norm': gain(ks[19], (DEPTH, D_MODEL)),
        'ffn2_w1': nrm(ks[20], (DEPTH, D_MODEL, D_FF), D_MODEL ** -0.5),
        'ffn2_w3': nrm(ks[21], (DEPTH, D_MODEL, D_FF), D_MODEL ** -0.5),
        'ffn2_w2': nrm(ks[22], (DEPTH, D_FF, D_MODEL), D_FF ** -0.5),
        'final_norm': gain(ks[23], (D_MODEL,)),
    }


def reference(x_prompt, x_sample, state_conv, state_gdn, state_ret, state_pool, ffn1_norm, ffn1_w1, ffn1_w3,
              ffn1_w2, mix_norm, w_in, conv_w, gdn_a_log, gdn_dt_bias, gdn_norm_w, pool_w, pool_scale, w_out,
              ffn2_norm, ffn2_w1, ffn2_w3, ffn2_w2, final_norm):
    weights = (ffn1_norm, ffn1_w1, ffn1_w3, ffn1_w2, mix_norm, w_in, conv_w, gdn_a_log, gdn_dt_bias,
               gdn_norm_w, pool_w, pool_scale, w_out, ffn2_norm, ffn2_w1, ffn2_w3, ffn2_w2, final_norm)
    bp = x_prompt.shape[0]
    zero_conv = jnp.zeros((DEPTH, bp, CONV_W - 1, GDN_CONV_DIM), x_prompt.dtype)
    zero_gdn = jnp.zeros((DEPTH, bp, GDN_HEADS, GDN_DK, GDN_DV), state_gdn.dtype)
    zero_ret = jnp.zeros((DEPTH, bp, RET_HEADS, RET_DK, RET_DV), state_ret.dtype)
    zero_pool = jnp.zeros((DEPTH, bp, POOL_BUF, POOL_WIDTH), x_prompt.dtype)
    y_prompt, conv_p, gdn_p, ret_p, pool_p = _trunk(x_prompt, 0, zero_conv, zero_gdn, zero_ret, zero_pool,
                                                    *weights)
    y_sample, conv_s, gdn_s, ret_s, pool_s = _trunk(x_sample, PAST_LEN, state_conv, state_gdn, state_ret,
                                                    state_pool, *weights)
    return (y_prompt, y_sample, conv_p, gdn_p, ret_p, pool_p, conv_s, gdn_s, ret_s, pool_s)
```

```python
import functools
import math

import jax
import jax.numpy as jnp
from jax import lax
from jax.experimental import pallas as pl
from jax.experimental.pallas import tpu as pltpu

CHUNK = 64
EPS = 1e-6
ROPE_BASE = 10000.0
POOL_WINDOWS = (2, 4, 8, 16)
PAST_LEN = 4096
LANES = 128
VMEM_LIMIT_BYTES = 56 * 1024 * 1024

f32 = jnp.float32
bf16 = jnp.bfloat16
HI = lax.Precision.HIGHEST


def _pick(n, candidates):
    for c in candidates:
        if n % c == 0:
            return c
    raise ValueError(f"no block size in {candidates} divides {n}")


def _params(sem):
    return pltpu.CompilerParams(dimension_semantics=sem, vmem_limit_bytes=VMEM_LIMIT_BYTES)


def _silu(x):
    return x * jax.nn.sigmoid(x)


def _dot(a, b):
    return jnp.dot(a, b, preferred_element_type=f32)


def _dot_hi(a, b):
    return jnp.dot(a, b, preferred_element_type=f32, precision=HI)


def _dot_nt(a, b):
    return lax.dot_general(a, b, (((1,), (1,)), ((), ())), preferred_element_type=f32)


def _dot_tn(a, b):
    return lax.dot_general(a, b, (((0,), (0,)), ((), ())), preferred_element_type=f32)


def _rms_kernel(x_ref, g_ref, o_ref):
    x = x_ref[...]
    y = x * lax.rsqrt(jnp.mean(x * x, axis=-1, keepdims=True) + EPS) * g_ref[...]
    o_ref[...] = y.astype(o_ref.dtype)


def _rmsnorm(x, g, out_dtype):
    m, d = x.shape
    bm = _pick(m, (512, 256, 128, 64))
    return pl.pallas_call(
        _rms_kernel,
        grid=(m // bm,),
        in_specs=[pl.BlockSpec((bm, d), lambda i: (i, 0)), pl.BlockSpec((1, d), lambda i: (0, 0))],
        out_specs=pl.BlockSpec((bm, d), lambda i: (i, 0)),
        out_shape=jax.ShapeDtypeStruct((m, d), out_dtype),
        compiler_params=_params(("parallel",)),
        name="rmsnorm",
    )(x, g.reshape(1, d))


def _up_kernel(h_ref, w1_ref, w3_ref, o_ref):
    h = h_ref[...]
    g = _dot(h, w1_ref[...])
    u = _dot(h, w3_ref[...])
    o_ref[...] = (_silu(g) * u).astype(o_ref.dtype)


def _ffn_up(h, w1, w3):
    m, d = h.shape
    f = w1.shape[1]
    bm = _pick(m, (1024, 512, 256, 128, 64))
    bn = _pick(f, (512, 256, 128))
    return pl.pallas_call(
        _up_kernel,
        grid=(m // bm, f // bn),
        in_specs=[
            pl.BlockSpec((bm, d), lambda i, j: (i, 0)),
            pl.BlockSpec((d, bn), lambda i, j: (0, j)),
            pl.BlockSpec((d, bn), lambda i, j: (0, j)),
        ],
        out_specs=pl.BlockSpec((bm, bn), lambda i, j: (i, j)),
        out_shape=jax.ShapeDtypeStruct((m, f), bf16),
        compiler_params=_params(("parallel", "arbitrary")),
        name="ffn_up",
    )(h, w1, w3)


def _mm_kernel(a_ref, w_ref, o_ref):
    o_ref[...] = _dot(a_ref[...], w_ref[...]).astype(o_ref.dtype)


def _matmul(a, w, out_dtype):
    m, k = a.shape
    n = w.shape[1]
    bm = _pick(m, (1024, 512, 256, 128, 64))
    bn = _pick(n, (512, 384, 256, 128))
    return pl.pallas_call(
        _mm_kernel,
        grid=(m // bm, n // bn),
        in_specs=[pl.BlockSpec((bm, k), lambda i, j: (i, 0)), pl.BlockSpec((k, bn), lambda i, j: (0, j))],
        out_specs=pl.BlockSpec((bm, bn), lambda i, j: (i, j)),
        out_shape=jax.ShapeDtypeStruct((m, n), out_dtype),
        compiler_params=_params(("parallel", "arbitrary")),
        name="in_proj",
    )(a, w)


def _down_kernel(a_ref, w_ref, x_ref, g_ref, xo_ref, ho_ref, *, scale, nk):
    k = pl.program_id(1)

    @pl.when(k == 0)
    def _():
        xo_ref[...] = x_ref[...]

    xo_ref[...] += scale * _dot(a_ref[...], w_ref[...])

    @pl.when(k == nk - 1)
    def _():
        x = xo_ref[...]
        y = x * lax.rsqrt(jnp.mean(x * x, axis=-1, keepdims=True) + EPS) * g_ref[...]
        ho_ref[...] = y.astype(ho_ref.dtype)


def _down_final_kernel(a_ref, w_ref, x_ref, g_ref, yo_ref, acc_ref, *, scale, nk):
    k = pl.program_id(1)

    @pl.when(k == 0)
    def _():
        acc_ref[...] = x_ref[...]

    acc_ref[...] += scale * _dot(a_ref[...], w_ref[...])

    @pl.when(k == nk - 1)
    def _():
        x = acc_ref[...]
        yo_ref[...] = x * lax.rsqrt(jnp.mean(x * x, axis=-1, keepdims=True) + EPS) * g_ref[...]


def _proj_residual_norm(a, w, x, g, scale, final=False):
    m, kdim = a.shape
    d = w.shape[1]
    bm = _pick(m, (256, 128, 64) if final else (512, 256, 128, 64))
    bk = _pick(kdim, (512, 256, 128))
    nk = kdim // bk
    in_specs = [
        pl.BlockSpec((bm, bk), lambda i, k: (i, k)),
        pl.BlockSpec((bk, d), lambda i, k: (k, 0)),
        pl.BlockSpec((bm, d), lambda i, k: (i, 0), pipeline_mode=pl.Buffered(1)),
        pl.BlockSpec((1, d), lambda i, k: (0, 0)),
    ]
    row_spec = pl.BlockSpec((bm, d), lambda i, k: (i, 0))
    if final:
        return pl.pallas_call(
            functools.partial(_down_final_kernel, scale=scale, nk=nk),
            grid=(m // bm, nk),
            in_specs=in_specs,
            out_specs=row_spec,
            out_shape=jax.ShapeDtypeStruct((m, d), f32),
            scratch_shapes=[pltpu.VMEM((bm, d), f32)],
            compiler_params=_params(("parallel", "arbitrary")),
            name="down_final",
        )(a, w, x, g.reshape(1, d))
    return pl.pallas_call(
        functools.partial(_down_kernel, scale=scale, nk=nk),
        grid=(m // bm, nk),
        in_specs=in_specs,
        out_specs=[row_spec, row_spec],
        out_shape=[jax.ShapeDtypeStruct((m, d), f32), jax.ShapeDtypeStruct((m, d), bf16)],
        compiler_params=_params(("parallel", "arbitrary")),
        name="down_proj",
    )(a, w, x, g.reshape(1, d))


class _Sched:
    def __init__(self, bp, lp, bs, ls):
        assert lp % CHUNK == 0 and ls % CHUNK == 0 and PAST_LEN % CHUNK == 0
        self.bp, self.np_, self.bs, self.ns = bp, lp // CHUNK, bs, ls // CHUNK
        self.n_prompt = self.bp * self.np_
        self.n_chunks = self.n_prompt + self.bs * self.ns
        self.n_seq = bp + bs

    def split(self, c):
        in_prompt = c < self.n_prompt
        cs = jnp.maximum(c - self.n_prompt, 0)
        seq = jnp.where(in_prompt, c // self.np_, self.bp + cs // self.ns)
        ci = jnp.where(in_prompt, c % self.np_, cs % self.ns)
        last = jnp.where(in_prompt, self.np_ - 1, self.ns - 1)
        return in_prompt, seq, ci, ci == last

    def seq(self, c):
        return self.split(c)[1]

    def pos_chunk(self, c):
        in_prompt, _, ci, _ = self.split(c)
        return jnp.where(in_prompt, 0, PAST_LEN // CHUNK) + ci


def _unit_lower_inverse(low, row, col):
    eye = (row == col).astype(f32)
    pair = (row // 2) == (col // 2)
    d = eye - jnp.where(pair, low, 0.0)
    b = 2
    while b < CHUNK:
        same_big = (row // (2 * b)) == (col // (2 * b))
        diff_small = (row // b) != (col // b)
        c = jnp.where(same_big & diff_small, low, 0.0)
        d = d - _dot_hi(_dot_hi(d, c), d)
        b *= 2
    return d


def _gdn_kernel(qkv_ref, z_ref, ab_ref, cinit_ref, sinit_ref, cw_ref, alog_ref, dtb_ref, nw_ref,
                o_ref, sout_ref, xp_ref, s_ref, gt_ref, *, sched, n_heads, dk, dv):
    c = pl.program_id(0)
    _, _, ci, is_last = sched.split(c)
    w = n_heads * dk
    hist = 8
    kw = cw_ref.shape[0]

    @pl.when(ci == 0)
    def _():
        xp_ref[0:hist, :] = jnp.zeros((hist, xp_ref.shape[1]), f32)
        xp_ref[hist - (kw - 1):hist, :] = cinit_ref[0]
        s_ref[...] = sinit_ref[0]

    @pl.when(ci != 0)
    def _():
        xp_ref[0:hist, :] = xp_ref[CHUNK:CHUNK + hist, :]

    xp_ref[hist:hist + CHUNK, :] = qkv_ref[...]

    ab = ab_ref[...]
    sp = jnp.maximum(ab + dtb_ref[...], 0.0) + jnp.log1p(jnp.exp(-jnp.abs(ab + dtb_ref[...])))
    g_all = -jnp.exp(alog_ref[...]) * sp
    beta_all = jax.nn.sigmoid(ab)

    r2 = lax.broadcasted_iota(jnp.int32, (2 * CHUNK, CHUNK), 0)
    c2 = lax.broadcasted_iota(jnp.int32, (2 * CHUNK, CHUNK), 1)
    gp = _dot_hi((c2 <= r2).astype(f32), g_all)
    gt_ref[...] = gp.T
    g_cum = gp[0:CHUNK, :]
    eg_all = jnp.exp(g_cum)
    g_last = g_cum[CHUNK - 1:CHUNK, :]
    tail_all = jnp.exp(g_last - g_cum)
    gblk_all = jnp.exp(g_last)

    row = lax.broadcasted_iota(jnp.int32, (CHUNK, CHUNK), 0)
    col = lax.broadcasted_iota(jnp.int32, (CHUNK, CHUNK), 1)
    causal = row >= col
    strict = row > col

    def conv(off):
        acc = xp_ref[hist - (kw - 1):hist - (kw - 1) + CHUNK, off:off + dk] * cw_ref[0:1, off:off + dk]
        for i in range(1, kw):
            r0 = hist - (kw - 1) + i
            acc = acc + xp_ref[r0:r0 + CHUNK, off:off + dk] * cw_ref[i:i + 1, off:off + dk]
        return _silu(acc)

    for h in range(n_heads):
        qc = conv(h * dk)
        kc = conv(w + h * dk)
        vc = conv(2 * w + h * dv)
        qn = qc * lax.rsqrt(jnp.sum(qc * qc, axis=-1, keepdims=True) + EPS) * (dk ** -0.5)
        kn = kc * lax.rsqrt(jnp.sum(kc * kc, axis=-1, keepdims=True) + EPS)

        gc = g_cum[:, h:h + 1]
        gr = gt_ref[h:h + 1, 0:CHUNK]
        bc = beta_all[:, n_heads + h:n_heads + h + 1]
        egc = eg_all[:, h:h + 1]
        decay = jnp.where(causal, jnp.exp(jnp.where(causal, gc - gr, 0.0)), 0.0)

        kb = kn.astype(bf16)
        kk = _dot_nt(kb, kb)
        low = jnp.where(strict, kk * decay * bc, 0.0)
        tinv = _unit_lower_inverse(low, row, col)
        rhs = jnp.concatenate([vc * bc, kn * (bc * egc)], axis=-1)
        sol = _dot_hi(tinv, rhs)
        u_base = sol[:, 0:dv]
        w_dec = sol[:, dv:dv + dk]
        qk = _dot_nt(qn.astype(bf16), kb) * decay

        s = s_ref[h]
        sb = s.astype(bf16)
        u = u_base - _dot(w_dec.astype(bf16), sb)
        ub = u.astype(bf16)
        o = _dot((qn * egc).astype(bf16), sb) + _dot(qk.astype(bf16), ub)
        s_ref[h] = s * gblk_all[:, h:h + 1] + _dot_tn((kn * tail_all[:, h:h + 1]).astype(bf16), ub)

        o = o * lax.rsqrt(jnp.mean(o * o, axis=-1, keepdims=True) + EPS) * nw_ref[...]
        o = o * _silu(z_ref[:, h * dv:(h + 1) * dv])
        o_ref[:, h * dv:(h + 1) * dv] = o.astype(o_ref.dtype)

    @pl.when(is_last)
    def _():
        sout_ref[0] = s_ref[...]


def _gdn_mixer(pa, pab, conv_init, s_init, conv_w, a_log, dt_bias, norm_w, sched):
    m = pa.shape[0]
    n_seq, n_heads, dk, dv = s_init.shape
    assert dk == dv == LANES and 2 * n_heads <= LANES
    w = n_heads * dk
    kw = conv_w.shape[0]
    pad = lambda t: jnp.pad(t.reshape(1, -1), ((0, 0), (0, LANES - t.size)))
    return pl.pallas_call(
        functools.partial(_gdn_kernel, sched=sched, n_heads=n_heads, dk=dk, dv=dv),
        grid=(sched.n_chunks,),
        in_specs=[
            pl.BlockSpec((CHUNK, 3 * w), lambda c: (c, 0)),
            pl.BlockSpec((CHUNK, w), lambda c: (c, 3)),
            pl.BlockSpec((CHUNK, LANES), lambda c: (c, 0)),
            pl.BlockSpec((1, kw - 1, 3 * w), lambda c: (sched.seq(c), 0, 0)),
            pl.BlockSpec((1, n_heads, dk, dv), lambda c: (sched.seq(c), 0, 0, 0)),
            pl.BlockSpec((kw, 3 * w), lambda c: (0, 0)),
            pl.BlockSpec((1, LANES), lambda c: (0, 0)),
            pl.BlockSpec((1, LANES), lambda c: (0, 0)),
            pl.BlockSpec((1, dv), lambda c: (0, 0)),
        ],
        out_specs=[
            pl.BlockSpec((CHUNK, w), lambda c: (c, 0)),
            pl.BlockSpec((1, n_heads, dk, dv), lambda c: (sched.seq(c), 0, 0, 0)),
        ],
        out_shape=[jax.ShapeDtypeStruct((m, w), bf16), jax.ShapeDtypeStruct(s_init.shape, f32)],
        scratch_shapes=[
            pltpu.VMEM((CHUNK + 8, 3 * w), f32),
            pltpu.VMEM((n_heads, dk, dv), f32),
            pltpu.VMEM((LANES, 2 * CHUNK), f32),
        ],
        compiler_params=_params(("arbitrary",)),
        name="gdn_mixer",
    )(pa, pa, pab, conv_init, s_init, conv_w, pad(a_log), pad(dt_bias), norm_w.reshape(1, dv))


def _ret_kernel(q_ref, k_ref, v_ref, g_ref, cos_ref, sin_ref, dmat_ref, qdec_ref, ktail_ref, gblk_ref, sinit_ref,
                o_ref, sout_ref, s_ref, *, sched, n_heads, dk, dv):
    c = pl.program_id(0)
    _, _, ci, is_last = sched.split(c)

    @pl.when(ci == 0)
    def _():
        s_ref[...] = sinit_ref[0]

    cos = cos_ref[...]
    sin = sin_ref[...]

    def rope(t):
        return t * cos + pltpu.roll(t, dk // 2, axis=1) * sin

    for h in range(n_heads):
        qr = rope(q_ref[:, h * dk:(h + 1) * dk])
        kr = rope(k_ref[:, h * dk:(h + 1) * dk]) * (dk ** -0.5)
        vb = v_ref[:, h * dv:(h + 1) * dv].astype(bf16)
        scores = _dot_nt(qr.astype(bf16), kr.astype(bf16)) * dmat_ref[h]
        s = s_ref[h]
        o = _dot(scores.astype(bf16), vb) + _dot((qr * qdec_ref[h]).astype(bf16), s.astype(bf16))
        s_ref[h] = s * gblk_ref[h] + _dot_tn((kr * ktail_ref[h]).astype(bf16), vb)
        oc = o - jnp.mean(o, axis=-1, keepdims=True)
        on = oc * lax.rsqrt(jnp.mean(oc * oc, axis=-1, keepdims=True) + EPS)
        o_ref[:, h * dv:(h + 1) * dv] = (on * _silu(g_ref[:, h * dv:(h + 1) * dv])).astype(o_ref.dtype)

    @pl.when(is_last)
    def _():
        sout_ref[0] = s_ref[...]


def _ret_tables(n_heads, dk, dv, n_pos):
    half = dk // 2
    inv_freq = ROPE_BASE ** (-jnp.arange(half, dtype=f32) / half)
    ang = jnp.arange(n_pos, dtype=jnp.int32).astype(f32)[:, None] * inv_freq[None, :]
    cos, sin = jnp.cos(ang), jnp.sin(ang)
    cos2 = jnp.concatenate([cos, cos], axis=-1)
    sin2 = jnp.concatenate([-sin, sin], axis=-1)
    lg = jnp.log1p(-jnp.exp2(-5.0 - jnp.arange(n_heads, dtype=f32)))
    idx = jnp.arange(CHUNK, dtype=f32)
    rel = idx[:, None] - idx[None, :]
    causal = rel >= 0
    dmat = jnp.where(causal, jnp.exp(jnp.where(causal, rel, 0.0) * lg[:, None, None]), 0.0)
    qdec = jnp.exp((idx + 1.0)[None, :] * lg[:, None])
    ktail = jnp.exp((CHUNK - 1.0 - idx)[None, :] * lg[:, None])
    gblk = jnp.exp(CHUNK * lg)
    bl = lambda t: jnp.broadcast_to(t[:, :, None], (n_heads, CHUNK, dk))
    return cos2, sin2, dmat, bl(qdec), bl(ktail), jnp.broadcast_to(gblk[:, None, None], (n_heads, 1, dv))


def _ret_mixer(pb, s_init, sched, n_pos):
    m = pb.shape[0]
    n_seq, n_heads, dk, dv = s_init.shape
    assert dk == LANES and dv == 2 * dk
    qw, vw = n_heads * dk, n_heads * dv
    cos2, sin2, dmat, qdec, ktail, gblk = _ret_tables(n_heads, dk, dv, n_pos)
    whole = lambda a: pl.BlockSpec(a.shape, lambda c: (0,) * a.ndim)
    return pl.pallas_call(
        functools.partial(_ret_kernel, sched=sched, n_heads=n_heads, dk=dk, dv=dv),
        grid=(sched.n_chunks,),
        in_specs=[
            pl.BlockSpec((CHUNK, qw), lambda c: (c, 0)),
            pl.BlockSpec((CHUNK, qw), lambda c: (c, 1)),
            pl.BlockSpec((CHUNK, vw), lambda c: (c, 1)),
            pl.BlockSpec((CHUNK, vw), lambda c: (c, 2)),
            pl.BlockSpec((CHUNK, dk), lambda c: (sched.pos_chunk(c), 0)),
            pl.BlockSpec((CHUNK, dk), lambda c: (sched.pos_chunk(c), 0)),
            whole(dmat), whole(qdec), whole(ktail), whole(gblk),
            pl.BlockSpec((1, n_heads, dk, dv), lambda c: (sched.seq(c), 0, 0, 0)),
        ],
        out_specs=[
            pl.BlockSpec((CHUNK, vw), lambda c: (c, 0)),
            pl.BlockSpec((1, n_heads, dk, dv), lambda c: (sched.seq(c), 0, 0, 0)),
        ],
        out_shape=[jax.ShapeDtypeStruct((m, vw), bf16), jax.ShapeDtypeStruct(s_init.shape, f32)],
        scratch_shapes=[pltpu.VMEM((n_heads, dk, dv), f32)],
        compiler_params=_params(("arbitrary",)),
        name="ret_mixer",
    )(pb, pb, pb, pb, cos2, sin2, dmat, qdec, ktail, gblk, s_init)


def _pool_kernel(u_ref, binit_ref, w_ref, scale_ref, o_ref, up_ref, *, sched, gc):
    c = pl.program_id(0)
    in_prompt, _, ci, _ = sched.split(c)
    hist = 16
    nbuf = binit_ref.shape[1]

    @pl.when(ci == 0)
    def _():
        up_ref[0:hist, :] = jnp.zeros((hist, up_ref.shape[1]), f32)
        up_ref[hist - nbuf:hist, :] = binit_ref[0]

    @pl.when(ci != 0)
    def _():
        up_ref[0:hist, :] = up_ref[CHUNK:CHUNK + hist, :]

    up_ref[hist:hist + CHUNK, :] = u_ref[...]

    pos = (jnp.where(in_prompt, 0, PAST_LEN) + ci * CHUNK
           + lax.broadcasted_iota(jnp.int32, (CHUNK, 1), 0))
    for gi, win_len in enumerate(POOL_WINDOWS):
        cols = slice(gi * gc, (gi + 1) * gc)
        cur = up_ref[hist:hist + CHUNK, cols]
        win = cur
        for j in range(1, win_len):
            win = win + up_ref[hist - j:hist - j + CHUNK, cols]
        cnt = jnp.minimum(pos + 1, win_len).astype(f32)
        d = win / cnt - cur
        y = _dot(d.astype(bf16), w_ref[gi]) * scale_ref[:, cols]
        o_ref[:, cols] = y.astype(o_ref.dtype)


def _pool_mixer(pc, buf_init, pool_w, pool_scale, sched):
    m, width = pc.shape
    n_grp, gc, _ = pool_w.shape
    nbuf = buf_init.shape[1]
    assert n_grp == len(POOL_WINDOWS) and nbuf == max(POOL_WINDOWS) - 1 and gc % LANES == 0
    return pl.pallas_call(
        functools.partial(_pool_kernel, sched=sched, gc=gc),
        grid=(sched.n_chunks,),
        in_specs=[
            pl.BlockSpec((CHUNK, width), lambda c: (c, 0)),
            pl.BlockSpec((1, nbuf, width), lambda c: (sched.seq(c), 0, 0)),
            pl.BlockSpec((n_grp, gc, gc), lambda c: (0, 0, 0)),
            pl.BlockSpec((1, width), lambda c: (0, 0)),
        ],
        out_specs=pl.BlockSpec((CHUNK, width), lambda c: (c, 0)),
        out_shape=jax.ShapeDtypeStruct((m, width), bf16),
        scratch_shapes=[pltpu.VMEM((CHUNK + 16, width), f32)],
        compiler_params=_params(("arbitrary",)),
        name="pool_mixer",
    )(pc, buf_init, pool_w.astype(bf16), pool_scale.reshape(1, width))


def _last_rows(p, sched, n_rows, width):
    lp, ls = sched.np_ * CHUNK, sched.ns * CHUNK
    mp = sched.bp * lp
    a = p[:mp].reshape(sched.bp, lp, -1)[:, lp - n_rows:, :width]
    b = p[mp:].reshape(sched.bs, ls, -1)[:, ls - n_rows:, :width]
    return a, b


def kernel(x_prompt, x_sample, state_conv, state_gdn, state_ret, state_pool, ffn1_norm, ffn1_w1, ffn1_w3, ffn1_w2,
           mix_norm, w_in, conv_w, gdn_a_log, gdn_dt_bias, gdn_norm_w, pool_w, pool_scale, w_out, ffn2_norm,
           ffn2_w1, ffn2_w3, ffn2_w2, final_norm):
    bp, lp, d = x_prompt.shape
    bs, ls, _ = x_sample.shape
    depth = ffn1_w1.shape[0]
    sched = _Sched(bp, lp, bs, ls)
    mp = bp * lp
    n_pos = max(lp, PAST_LEN + ls)

    _, _, ha, dka, dva = state_gdn.shape
    _, _, hb, dkb, dvb = state_ret.shape
    wa, qkb, wb = ha * dva, hb * dkb, hb * dvb
    pool_width = state_pool.shape[-1]
    in_sizes = (2 * ha * dka + wa, wa, ha, ha, qkb, qkb, wb, wb, pool_width)
    assert sum(in_sizes) == w_in.shape[-1]
    offs = [0]
    for s in in_sizes:
        offs.append(offs[-1] + s)
    cols = lambda t, i, j: t[..., offs[i]:offs[j]]

    f = ffn1_w1.shape[-1]
    fpad = -f % 512
    cast_up = lambda t: jnp.pad(t, ((0, 0), (0, 0), (0, fpad))).astype(bf16)
    cast_dn = lambda t: jnp.pad(t, ((0, 0), (0, fpad), (0, 0))).astype(bf16)
    f1w1, f1w3, f1w2 = cast_up(ffn1_w1), cast_up(ffn1_w3), cast_dn(ffn1_w2)
    f2w1, f2w3, f2w2 = cast_up(ffn2_w1), cast_up(ffn2_w3), cast_dn(ffn2_w2)
    w_a = cols(w_in, 0, 2).astype(bf16)
    w_ab = jnp.pad(cols(w_in, 2, 4), ((0, 0), (0, 0), (0, LANES - 2 * ha))).astype(bf16)
    w_b = cols(w_in, 4, 8).astype(bf16)
    w_c = cols(w_in, 8, 9).astype(bf16)
    w_o = w_out.astype(bf16)

    def stack_state(st):
        return jnp.concatenate([jnp.zeros((st.shape[0], bp) + st.shape[2:], st.dtype), st], axis=1)

    conv0, gdn0, ret0, pool0 = map(stack_state, (state_conv, state_gdn, state_ret, state_pool))

    x = jnp.concatenate([x_prompt.reshape(mp, d), x_sample.reshape(bs * ls, d)], axis=0)
    h = _rmsnorm(x, ffn1_norm[0], bf16)
    outs = {k: [] for k in ("conv_p", "gdn_p", "ret_p", "pool_p", "conv_s", "gdn_s", "ret_s", "pool_s")}
    y = None
    for l in range(depth):
        x, h = _proj_residual_norm(_ffn_up(h, f1w1[l], f1w3[l]), f1w2[l], x, mix_norm[l], 0.5)
        pa = _matmul(h, w_a[l], f32)
        pab = _matmul(h, w_ab[l], f32)
        pb = _matmul(h, w_b[l], f32)
        pc = _matmul(h, w_c[l], f32)
        oa, sa = _gdn_mixer(pa, pab, conv0[l], gdn0[l], conv_w[l], gdn_a_log[l], gdn_dt_bias[l], gdn_norm_w[l], sched)
        ob, sb = _ret_mixer(pb, ret0[l], sched, n_pos)
        oc = _pool_mixer(pc, pool0[l], pool_w[l], pool_scale[l], sched)
        mix = jnp.concatenate([oa, ob, oc], axis=-1)
        x, h = _proj_residual_norm(mix, w_o[l], x, ffn2_norm[l], 1.0)
        hmid = _ffn_up(h, f2w1[l], f2w3[l])
        if l + 1 < depth:
            x, h = _proj_residual_norm(hmid, f2w2[l], x, ffn1_norm[l + 1], 0.5)
        else:
            y = _proj_residual_norm(hmid, f2w2[l], x, final_norm, 0.5, final=True)
        cp, cs = _last_rows(pa, sched, conv_w.shape[1] - 1, in_sizes[0])
        pp, ps = _last_rows(pc, sched, state_pool.shape[2], pool_width)
        for k, v in (("conv_p", cp), ("conv_s", cs), ("pool_p", pp), ("pool_s", ps), ("gdn_p", sa[:bp]),
                     ("gdn_s", sa[bp:]), ("ret_p", sb[:bp]), ("ret_s", sb[bp:])):
            outs[k].append(v)

    st = {k: jnp.stack(v) for k, v in outs.items()}
    return (y[:mp].reshape(bp, lp, d), y[mp:].reshape(bs, ls, d), st["conv_p"], st["gdn_p"], st["ret_p"],
            st["pool_p"], st["conv_s"], st["gdn_s"], st["ret_s"], st["pool_s"])
```

```python
import functools

import jax
import jax.numpy as jnp
from jax import lax
from jax.experimental import pallas as pl
from jax.experimental.pallas import tpu as pltpu

CHUNK = 64
EPS = 1e-6
ROPE_BASE = 10000.0
POOL_WINDOWS = (2, 4, 8, 16)
PAST_LEN = 4096
LANES = 128
VMEM_LIMIT_BYTES = 56 * 1024 * 1024

f32 = jnp.float32
bf16 = jnp.bfloat16
HI = lax.Precision.HIGHEST


def _pick(n, candidates):
    for c in candidates:
        if n % c == 0:
            return c
    raise ValueError(f"no block size in {candidates} divides {n}")


def _params(sem):
    return pltpu.CompilerParams(dimension_semantics=sem, vmem_limit_bytes=VMEM_LIMIT_BYTES)


def _silu(x):
    return x * jax.nn.sigmoid(x)


def _dot(a, b):
    return jnp.dot(a, b, preferred_element_type=f32)


def _dot_hi(a, b):
    return jnp.dot(a, b, preferred_element_type=f32, precision=HI)


def _dot_nt(a, b):
    return lax.dot_general(a, b, (((1,), (1,)), ((), ())), preferred_element_type=f32)


def _dot_tn(a, b):
    return lax.dot_general(a, b, (((0,), (0,)), ((), ())), preferred_element_type=f32)


def _row_rsqrt(ss_ref, d_model):
    return lax.rsqrt(ss_ref[:, 0:1] / d_model + EPS)


def _norm_prep_kernel(x_ref, g_ref, hq_ref, ss_ref):
    x = x_ref[...]
    hq_ref[...] = (x * g_ref[...]).astype(hq_ref.dtype)
    ss_ref[...] = jnp.broadcast_to(jnp.sum(x * x, axis=-1, keepdims=True), ss_ref.shape)


def _norm_prep(x, g):
    m, d = x.shape
    bm = _pick(m, (512, 256, 128, 64))
    return pl.pallas_call(
        _norm_prep_kernel,
        grid=(m // bm,),
        in_specs=[pl.BlockSpec((bm, d), lambda i: (i, 0)), pl.BlockSpec((1, d), lambda i: (0, 0))],
        out_specs=[pl.BlockSpec((bm, d), lambda i: (i, 0)), pl.BlockSpec((bm, LANES), lambda i: (i, 0))],
        out_shape=[jax.ShapeDtypeStruct((m, d), bf16), jax.ShapeDtypeStruct((m, LANES), f32)],
        compiler_params=_params(("parallel",)),
        name="norm_prep",
    )(x, g.reshape(1, d))


def _final_norm_kernel(x_ref, g_ref, o_ref):
    x = x_ref[...]
    o_ref[...] = x * lax.rsqrt(jnp.mean(x * x, axis=-1, keepdims=True) + EPS) * g_ref[...]


def _final_norm(x, g, row0, n_rows):
    d = x.shape[1]
    bm = _pick(n_rows, (512, 256, 128, 64))
    assert row0 % bm == 0
    b0 = row0 // bm
    return pl.pallas_call(
        _final_norm_kernel,
        grid=(n_rows // bm,),
        in_specs=[pl.BlockSpec((bm, d), lambda i: (i + b0, 0)), pl.BlockSpec((1, d), lambda i: (0, 0))],
        out_specs=pl.BlockSpec((bm, d), lambda i: (i, 0)),
        out_shape=jax.ShapeDtypeStruct((n_rows, d), f32),
        compiler_params=_params(("parallel",)),
        name="final_norm",
    )(x, g.reshape(1, d))


def _up_kernel(h_ref, ss_ref, w1_ref, w3_ref, o_ref, *, d_model, scale):
    h = h_ref[...]
    r = _row_rsqrt(ss_ref, d_model)
    g = _dot(h, w1_ref[...]) * r
    u = _dot(h, w3_ref[...]) * r
    o_ref[...] = (scale * _silu(g) * u).astype(o_ref.dtype)


def _ffn_up(hq, ss, w1, w3, scale):
    m, d = hq.shape
    f = w1.shape[1]
    bm = _pick(m, (1024, 512, 256, 128, 64))
    bn = _pick(f, (512, 256, 128))
    return pl.pallas_call(
        functools.partial(_up_kernel, d_model=d, scale=scale),
        grid=(m // bm, f // bn),
        in_specs=[
            pl.BlockSpec((bm, d), lambda i, j: (i, 0)),
            pl.BlockSpec((bm, LANES), lambda i, j: (i, 0)),
            pl.BlockSpec((d, bn), lambda i, j: (0, j)),
            pl.BlockSpec((d, bn), lambda i, j: (0, j)),
        ],
        out_specs=pl.BlockSpec((bm, bn), lambda i, j: (i, j)),
        out_shape=jax.ShapeDtypeStruct((m, f), bf16),
        compiler_params=_params(("parallel", "arbitrary")),
        name="ffn_up",
    )(hq, ss, w1, w3)


def _in_proj_kernel(h_ref, ss_ref, w_ref, o_ref, *, d_model):
    o_ref[...] = _dot(h_ref[...], w_ref[...]) * _row_rsqrt(ss_ref, d_model)


def _in_proj(hq, ss, w):
    m, d = hq.shape
    n = w.shape[1]
    bm = _pick(m, (1024, 512, 256, 128, 64))
    bn = _pick(n, (512, 384, 256, 128))
    return pl.pallas_call(
        functools.partial(_in_proj_kernel, d_model=d),
        grid=(m // bm, n // bn),
        in_specs=[
            pl.BlockSpec((bm, d), lambda i, j: (i, 0)),
            pl.BlockSpec((bm, LANES), lambda i, j: (i, 0)),
            pl.BlockSpec((d, bn), lambda i, j: (0, j)),
        ],
        out_specs=pl.BlockSpec((bm, bn), lambda i, j: (i, j)),
        out_shape=jax.ShapeDtypeStruct((m, n), f32),
        compiler_params=_params(("parallel", "arbitrary")),
        name="in_proj",
    )(hq, ss, w)


def _proj_residual_kernel(*refs, n_in, with_norm):
    a_refs, w_refs = refs[:n_in], refs[n_in:2 * n_in]
    x_ref = refs[2 * n_in]
    acc = _dot(a_refs[0][...], w_refs[0][...])
    for a_ref, w_ref in zip(a_refs[1:], w_refs[1:]):
        acc = acc + _dot(a_ref[...], w_ref[...])
    xn = x_ref[...] + acc
    if not with_norm:
        refs[2 * n_in + 1][...] = xn
        return
    g_ref, xo_ref, hq_ref, ss_ref = refs[2 * n_in + 1:]
    xo_ref[...] = xn
    hq_ref[...] = (xn * g_ref[...]).astype(hq_ref.dtype)
    part = jnp.broadcast_to(jnp.sum(xn * xn, axis=-1, keepdims=True), ss_ref.shape)
    j = pl.program_id(1)

    @pl.when(j == 0)
    def _():
        ss_ref[...] = part

    @pl.when(j != 0)
    def _():
        ss_ref[...] += part


def _proj_residual(a_list, w_list, x, g, bm_cands, bn_cands):
    m, d = x.shape
    bm = _pick(m, bm_cands)
    bn = _pick(d, bn_cands)
    with_norm = g is not None
    tile = pl.BlockSpec((bm, bn), lambda i, j: (i, j))
    in_specs = [pl.BlockSpec((bm, a.shape[1]), lambda i, j: (i, 0)) for a in a_list]
    in_specs += [pl.BlockSpec((w.shape[0], bn), lambda i, j: (0, j)) for w in w_list]
    in_specs.append(tile)
    args = list(a_list) + list(w_list) + [x]
    if with_norm:
        in_specs.append(pl.BlockSpec((1, bn), lambda i, j: (0, j)))
        args.append(g.reshape(1, d))
        out_specs = [tile, tile, pl.BlockSpec((bm, LANES), lambda i, j: (i, 0))]
        out_shape = [jax.ShapeDtypeStruct((m, d), f32), jax.ShapeDtypeStruct((m, d), bf16),
                     jax.ShapeDtypeStruct((m, LANES), f32)]
    else:
        out_specs, out_shape = tile, jax.ShapeDtypeStruct((m, d), f32)
    return pl.pallas_call(
        functools.partial(_proj_residual_kernel, n_in=len(a_list), with_norm=with_norm),
        grid=(m // bm, d // bn),
        in_specs=in_specs,
        out_specs=out_specs,
        out_shape=out_shape,
        compiler_params=_params(("parallel", "arbitrary")),
        name="proj_residual",
    )(*args)


class _Sched:
    def __init__(self, bp, lp, bs, ls):
        assert lp % CHUNK == 0 and ls % CHUNK == 0 and PAST_LEN % CHUNK == 0
        self.bp, self.np_, self.bs, self.ns = bp, lp // CHUNK, bs, ls // CHUNK
        self.n_prompt = self.bp * self.np_
        self.n_chunks = self.n_prompt + self.bs * self.ns
        self.n_seq = bp + bs

    def split(self, c):
        in_prompt = c < self.n_prompt
        cs = jnp.maximum(c - self.n_prompt, 0)
        seq = jnp.where(in_prompt, c // self.np_, self.bp + cs // self.ns)
        ci = jnp.where(in_prompt, c % self.np_, cs % self.ns)
        last = jnp.where(in_prompt, self.np_ - 1, self.ns - 1)
        return in_prompt, seq, ci, ci == last

    def seq(self, c):
        return self.split(c)[1]

    def pos_chunk(self, c):
        in_prompt, _, ci, _ = self.split(c)
        return jnp.where(in_prompt, 0, PAST_LEN // CHUNK) + ci


def _strict_lower_inverse_minus_eye(lows, row, col):
    pair = (row // 2) == (col // 2)
    es = [-jnp.where(pair, low, 0.0) for low in lows]
    b = 2
    while b < CHUNK:
        mask = ((row // (2 * b)) == (col // (2 * b))) & ((row // b) != (col // b))
        cs = [jnp.where(mask, low, 0.0) for low in lows]
        xs = [c + _dot(e.astype(bf16), c.astype(bf16)) for e, c in zip(es, cs)]
        ys = [x + _dot(x.astype(bf16), e.astype(bf16)) for x, e in zip(xs, es)]
        es = [e - y for e, y in zip(es, ys)]
        b *= 2
    return es


def _gdn_kernel(qkv_ref, z_ref, ab_ref, cinit_ref, sinit_ref, cw_ref, alog_ref, dtb_ref, nw_ref,
                o_ref, sout_ref, xp_ref, s_ref, gt_ref, *, sched, n_heads, dk, dv):
    c = pl.program_id(0)
    _, _, ci, is_last = sched.split(c)
    w = n_heads * dk
    hist = 8
    kw = cw_ref.shape[0]

    @pl.when(ci == 0)
    def _():
        xp_ref[0:hist, :] = jnp.zeros((hist, xp_ref.shape[1]), f32)
        xp_ref[hist - (kw - 1):hist, :] = cinit_ref[0]
        s_ref[...] = sinit_ref[0]

    @pl.when(ci != 0)
    def _():
        xp_ref[0:hist, :] = xp_ref[CHUNK:CHUNK + hist, :]

    xp_ref[hist:hist + CHUNK, :] = qkv_ref[...]

    ab = ab_ref[...]
    sp = jnp.maximum(ab + dtb_ref[...], 0.0) + jnp.log1p(jnp.exp(-jnp.abs(ab + dtb_ref[...])))
    g_all = -jnp.exp(alog_ref[...]) * sp
    beta_all = jax.nn.sigmoid(ab)

    r2 = lax.broadcasted_iota(jnp.int32, (2 * CHUNK, CHUNK), 0)
    c2 = lax.broadcasted_iota(jnp.int32, (2 * CHUNK, CHUNK), 1)
    gp = _dot_hi((c2 <= r2).astype(f32), g_all)
    gt_ref[...] = gp.T
    g_cum = gp[0:CHUNK, :]
    eg_all = jnp.exp(g_cum)
    g_last = g_cum[CHUNK - 1:CHUNK, :]
    tail_all = jnp.exp(g_last - g_cum)
    gblk_all = jnp.exp(g_last)

    row = lax.broadcasted_iota(jnp.int32, (CHUNK, CHUNK), 0)
    col = lax.broadcasted_iota(jnp.int32, (CHUNK, CHUNK), 1)
    causal = row >= col
    strict = row > col

    def conv(off):
        acc = xp_ref[hist - (kw - 1):hist - (kw - 1) + CHUNK, off:off + dk] * cw_ref[0:1, off:off + dk]
        for i in range(1, kw):
            r0 = hist - (kw - 1) + i
            acc = acc + xp_ref[r0:r0 + CHUNK, off:off + dk] * cw_ref[i:i + 1, off:off + dk]
        return _silu(acc)

    def l2n(t):
        return t * lax.rsqrt(jnp.sum(t * t, axis=-1, keepdims=True) + EPS)

    heads = range(n_heads)
    kn = [l2n(conv(w + h * dk)) for h in heads]
    kb = [t.astype(bf16) for t in kn]
    kk = [_dot_nt(t, t) for t in kb]
    qn = [l2n(conv(h * dk)) * (dk ** -0.5) for h in heads]
    qk = [_dot_nt(qn[h].astype(bf16), kb[h]) for h in heads]
    bc = [beta_all[:, n_heads + h:n_heads + h + 1] for h in heads]
    egc = [eg_all[:, h:h + 1] for h in heads]
    decay = [jnp.where(causal, jnp.exp(jnp.where(causal, g_cum[:, h:h + 1] - gt_ref[h:h + 1, 0:CHUNK], 0.0)), 0.0)
             for h in heads]
    lows = [jnp.where(strict, kk[h] * decay[h] * bc[h], 0.0) for h in heads]
    es = _strict_lower_inverse_minus_eye(lows, row, col)
    rhs = [jnp.concatenate([conv(2 * w + h * dv) * bc[h], kn[h] * (bc[h] * egc[h])], axis=-1) for h in heads]
    sol = [rhs[h] + _dot(es[h].astype(bf16), rhs[h].astype(bf16)) for h in heads]

    sb = [s_ref[h].astype(bf16) for h in heads]
    ws = [_dot(sol[h][:, dv:dv + dk].astype(bf16), sb[h]) for h in heads]
    ub = [(sol[h][:, 0:dv] - ws[h]).astype(bf16) for h in heads]
    qs = [_dot((qn[h] * egc[h]).astype(bf16), sb[h]) for h in heads]
    qu = [_dot((qk[h] * decay[h]).astype(bf16), ub[h]) for h in heads]
    ku = [_dot_tn((kn[h] * tail_all[:, h:h + 1]).astype(bf16), ub[h]) for h in heads]
    for h in heads:
        s_ref[h] = s_ref[h] * gblk_all[:, h:h + 1] + ku[h]
        o = qs[h] + qu[h]
        o = o * lax.rsqrt(jnp.mean(o * o, axis=-1, keepdims=True) + EPS) * nw_ref[...]
        o = o * _silu(z_ref[:, h * dv:(h + 1) * dv])
        o_ref[:, h * dv:(h + 1) * dv] = o.astype(o_ref.dtype)

    @pl.when(is_last)
    def _():
        sout_ref[0] = s_ref[...]


def _gdn_mixer(pa, pab, conv_init, s_init, conv_w, a_log, dt_bias, norm_w, sched):
    m = pa.shape[0]
    n_seq, n_heads, dk, dv = s_init.shape
    assert dk == dv == LANES and 2 * n_heads <= LANES
    w = n_heads * dk
    kw = conv_w.shape[0]
    pad = lambda t: jnp.pad(t.reshape(1, -1), ((0, 0), (0, LANES - t.size)))
    return pl.pallas_call(
        functools.partial(_gdn_kernel, sched=sched, n_heads=n_heads, dk=dk, dv=dv),
        grid=(sched.n_chunks,),
        in_specs=[
            pl.BlockSpec((CHUNK, 3 * w), lambda c: (c, 0)),
            pl.BlockSpec((CHUNK, w), lambda c: (c, 3)),
            pl.BlockSpec((CHUNK, LANES), lambda c: (c, 0)),
            pl.BlockSpec((1, kw - 1, 3 * w), lambda c: (sched.seq(c), 0, 0)),
            pl.BlockSpec((1, n_heads, dk, dv), lambda c: (sched.seq(c), 0, 0, 0)),
            pl.BlockSpec((kw, 3 * w), lambda c: (0, 0)),
            pl.BlockSpec((1, LANES), lambda c: (0, 0)),
            pl.BlockSpec((1, LANES), lambda c: (0, 0)),
            pl.BlockSpec((1, dv), lambda c: (0, 0)),
        ],
        out_specs=[
            pl.BlockSpec((CHUNK, w), lambda c: (c, 0)),
            pl.BlockSpec((1, n_heads, dk, dv), lambda c: (sched.seq(c), 0, 0, 0)),
        ],
        out_shape=[jax.ShapeDtypeStruct((m, w), bf16), jax.ShapeDtypeStruct(s_init.shape, f32)],
        scratch_shapes=[
            pltpu.VMEM((CHUNK + 8, 3 * w), f32),
            pltpu.VMEM((n_heads, dk, dv), f32),
            pltpu.VMEM((LANES, 2 * CHUNK), f32),
        ],
        compiler_params=_params(("arbitrary",)),
        name="gdn_mixer",
    )(pa, pa, pab, conv_init, s_init, conv_w, pad(a_log), pad(dt_bias), norm_w.reshape(1, dv))


def _ret_kernel(q_ref, k_ref, v_ref, g_ref, cos_ref, sin_ref, dmat_ref, qdec_ref, ktail_ref, gblk_ref, sinit_ref,
                o_ref, sout_ref, s_ref, *, sched, n_heads, dk, dv):
    c = pl.program_id(0)
    _, _, ci, is_last = sched.split(c)

    @pl.when(ci == 0)
    def _():
        s_ref[...] = sinit_ref[0]

    cos = cos_ref[...]
    sin = sin_ref[...]

    def rope(t):
        return t * cos + pltpu.roll(t, dk // 2, axis=1) * sin

    heads = range(n_heads)
    qr = [rope(q_ref[:, h * dk:(h + 1) * dk]) for h in heads]
    kr = [rope(k_ref[:, h * dk:(h + 1) * dk]) * (dk ** -0.5) for h in heads]
    vb = [v_ref[:, h * dv:(h + 1) * dv].astype(bf16) for h in heads]
    scores = [_dot_nt(qr[h].astype(bf16), kr[h].astype(bf16)) * dmat_ref[h] for h in heads]
    cross = [_dot((qr[h] * qdec_ref[h]).astype(bf16), s_ref[h].astype(bf16)) for h in heads]
    inner = [_dot(scores[h].astype(bf16), vb[h]) for h in heads]
    kv = [_dot_tn((kr[h] * ktail_ref[h]).astype(bf16), vb[h]) for h in heads]
    for h in heads:
        s_ref[h] = s_ref[h] * gblk_ref[h] + kv[h]
        o = inner[h] + cross[h]
        oc = o - jnp.mean(o, axis=-1, keepdims=True)
        on = oc * lax.rsqrt(jnp.mean(oc * oc, axis=-1, keepdims=True) + EPS)
        o_ref[:, h * dv:(h + 1) * dv] = (on * _silu(g_ref[:, h * dv:(h + 1) * dv])).astype(o_ref.dtype)

    @pl.when(is_last)
    def _():
        sout_ref[0] = s_ref[...]


def _ret_tables(n_heads, dk, dv, n_pos):
    half = dk // 2
    inv_freq = ROPE_BASE ** (-jnp.arange(half, dtype=f32) / half)
    ang = jnp.arange(n_pos, dtype=jnp.int32).astype(f32)[:, None] * inv_freq[None, :]
    cos, sin = jnp.cos(ang), jnp.sin(ang)
    cos2 = jnp.concatenate([cos, cos], axis=-1)
    sin2 = jnp.concatenate([-sin, sin], axis=-1)
    lg = jnp.log1p(-jnp.exp2(-5.0 - jnp.arange(n_heads, dtype=f32)))
    idx = jnp.arange(CHUNK, dtype=f32)
    rel = idx[:, None] - idx[None, :]
    causal = rel >= 0
    dmat = jnp.where(causal, jnp.exp(jnp.where(causal, rel, 0.0) * lg[:, None, None]), 0.0)
    qdec = jnp.exp((idx + 1.0)[None, :] * lg[:, None])
    ktail = jnp.exp((CHUNK - 1.0 - idx)[None, :] * lg[:, None])
    gblk = jnp.exp(CHUNK * lg)
    bl = lambda t: jnp.broadcast_to(t[:, :, None], (n_heads, CHUNK, dk))
    return cos2, sin2, dmat, bl(qdec), bl(ktail), jnp.broadcast_to(gblk[:, None, None], (n_heads, 1, dv))


def _ret_mixer(pb, s_init, sched, n_pos):
    m = pb.shape[0]
    n_seq, n_heads, dk, dv = s_init.shape
    assert dk == LANES and dv == 2 * dk
    qw, vw = n_heads * dk, n_heads * dv
    cos2, sin2, dmat, qdec, ktail, gblk = _ret_tables(n_heads, dk, dv, n_pos)
    whole = lambda a: pl.BlockSpec(a.shape, lambda c: (0,) * a.ndim)
    return pl.pallas_call(
        functools.partial(_ret_kernel, sched=sched, n_heads=n_heads, dk=dk, dv=dv),
        grid=(sched.n_chunks,),
        in_specs=[
            pl.BlockSpec((CHUNK, qw), lambda c: (c, 0)),
            pl.BlockSpec((CHUNK, qw), lambda c: (c, 1)),
            pl.BlockSpec((CHUNK, vw), lambda c: (c, 1)),
            pl.BlockSpec((CHUNK, vw), lambda c: (c, 2)),
            pl.BlockSpec((CHUNK, dk), lambda c: (sched.pos_chunk(c), 0)),
            pl.BlockSpec((CHUNK, dk), lambda c: (sched.pos_chunk(c), 0)),
            whole(dmat), whole(qdec), whole(ktail), whole(gblk),
            pl.BlockSpec((1, n_heads, dk, dv), lambda c: (sched.seq(c), 0, 0, 0)),
        ],
        out_specs=[
            pl.BlockSpec((CHUNK, vw), lambda c: (c, 0)),
            pl.BlockSpec((1, n_heads, dk, dv), lambda c: (sched.seq(c), 0, 0, 0)),
        ],
        out_shape=[jax.ShapeDtypeStruct((m, vw), bf16), jax.ShapeDtypeStruct(s_init.shape, f32)],
        scratch_shapes=[pltpu.VMEM((n_heads, dk, dv), f32)],
        compiler_params=_params(("arbitrary",)),
        name="ret_mixer",
    )(pb, pb, pb, pb, cos2, sin2, dmat, qdec, ktail, gblk, s_init)


def _pool_kernel(u_ref, binit_ref, w_ref, scale_ref, o_ref, up_ref, *, sched, gc):
    c = pl.program_id(0)
    in_prompt, _, ci, _ = sched.split(c)
    hist = 16
    nbuf = binit_ref.shape[1]

    @pl.when(ci == 0)
    def _():
        up_ref[0:hist, :] = jnp.zeros((hist, up_ref.shape[1]), f32)
        up_ref[hist - nbuf:hist, :] = binit_ref[0]

    @pl.when(ci != 0)
    def _():
        up_ref[0:hist, :] = up_ref[CHUNK:CHUNK + hist, :]

    up_ref[hist:hist + CHUNK, :] = u_ref[...]

    pos = (jnp.where(in_prompt, 0, PAST_LEN) + ci * CHUNK
           + lax.broadcasted_iota(jnp.int32, (CHUNK, 1), 0))
    for gi, win_len in enumerate(POOL_WINDOWS):
        cols = slice(gi * gc, (gi + 1) * gc)
        cur = up_ref[hist:hist + CHUNK, cols]
        win = cur
        for j in range(1, win_len):
            win = win + up_ref[hist - j:hist - j + CHUNK, cols]
        cnt = jnp.minimum(pos + 1, win_len).astype(f32)
        d = win / cnt - cur
        y = _dot(d.astype(bf16), w_ref[gi]) * scale_ref[:, cols]
        o_ref[:, cols] = y.astype(o_ref.dtype)


def _pool_mixer(pc, buf_init, pool_w, pool_scale, sched):
    m, width = pc.shape
    n_grp, gc, _ = pool_w.shape
    nbuf = buf_init.shape[1]
    assert n_grp == len(POOL_WINDOWS) and nbuf == max(POOL_WINDOWS) - 1 and gc % LANES == 0
    return pl.pallas_call(
        functools.partial(_pool_kernel, sched=sched, gc=gc),
        grid=(sched.n_chunks,),
        in_specs=[
            pl.BlockSpec((CHUNK, width), lambda c: (c, 0)),
            pl.BlockSpec((1, nbuf, width), lambda c: (sched.seq(c), 0, 0)),
            pl.BlockSpec((n_grp, gc, gc), lambda c: (0, 0, 0)),
            pl.BlockSpec((1, width), lambda c: (0, 0)),
        ],
        out_specs=pl.BlockSpec((CHUNK, width), lambda c: (c, 0)),
        out_shape=jax.ShapeDtypeStruct((m, width), bf16),
        scratch_shapes=[pltpu.VMEM((CHUNK + 16, width), f32)],
        compiler_params=_params(("arbitrary",)),
        name="pool_mixer",
    )(pc, buf_init, pool_w.astype(bf16), pool_scale.reshape(1, width))


def _last_rows(p, sched, n_rows, width):
    lp, ls = sched.np_ * CHUNK, sched.ns * CHUNK
    mp = sched.bp * lp
    a = p[:mp].reshape(sched.bp, lp, -1)[:, lp - n_rows:, :width]
    b = p[mp:].reshape(sched.bs, ls, -1)[:, ls - n_rows:, :width]
    return a, b


def kernel(x_prompt, x_sample, state_conv, state_gdn, state_ret, state_pool, ffn1_norm, ffn1_w1, ffn1_w3, ffn1_w2,
           mix_norm, w_in, conv_w, gdn_a_log, gdn_dt_bias, gdn_norm_w, pool_w, pool_scale, w_out, ffn2_norm,
           ffn2_w1, ffn2_w3, ffn2_w2, final_norm):
    bp, lp, d = x_prompt.shape
    bs, ls, _ = x_sample.shape
    depth = ffn1_w1.shape[0]
    sched = _Sched(bp, lp, bs, ls)
    mp, ms = bp * lp, bs * ls
    n_pos = max(lp, PAST_LEN + ls)

    _, _, ha, dka, dva = state_gdn.shape
    _, _, hb, dkb, dvb = state_ret.shape
    wa, qkb, wb = ha * dva, hb * dkb, hb * dvb
    pool_width = state_pool.shape[-1]
    in_sizes = (2 * ha * dka + wa, wa, ha, ha, qkb, qkb, wb, wb, pool_width)
    assert sum(in_sizes) == w_in.shape[-1]
    offs = [0]
    for s in in_sizes:
        offs.append(offs[-1] + s)
    cols = lambda t, i, j: t[..., offs[i]:offs[j]]

    f = ffn1_w1.shape[-1]
    fpad = -f % 512
    cast_up = lambda t: jnp.pad(t, ((0, 0), (0, 0), (0, fpad))).astype(bf16)
    cast_dn = lambda t: jnp.pad(t, ((0, 0), (0, fpad), (0, 0))).astype(bf16)
    f1w1, f1w3, f1w2 = cast_up(ffn1_w1), cast_up(ffn1_w3), cast_dn(ffn1_w2)
    f2w1, f2w3, f2w2 = cast_up(ffn2_w1), cast_up(ffn2_w3), cast_dn(ffn2_w2)
    w_a = cols(w_in, 0, 2).astype(bf16)
    w_ab = jnp.pad(cols(w_in, 2, 4), ((0, 0), (0, 0), (0, LANES - 2 * ha))).astype(bf16)
    w_b = cols(w_in, 4, 8).astype(bf16)
    w_c = cols(w_in, 8, 9).astype(bf16)
    w_o = w_out.astype(bf16)
    w_oa, w_ob, w_oc = w_o[:, :wa], w_o[:, wa:wa + wb], w_o[:, wa + wb:]

    def stack_state(st):
        return jnp.concatenate([jnp.zeros((st.shape[0], bp) + st.shape[2:], st.dtype), st], axis=1)

    conv0, gdn0, ret0, pool0 = map(stack_state, (state_conv, state_gdn, state_ret, state_pool))

    ffn_tiles = ((512, 256, 128, 64), (512, 256, 128))
    out_tiles = ((1024, 512, 256, 128, 64), (512, 256, 128))

    x = jnp.concatenate([x_prompt.reshape(mp, d), x_sample.reshape(ms, d)], axis=0)
    hq, ss = _norm_prep(x, ffn1_norm[0])
    outs = {k: [] for k in ("conv_p", "gdn_p", "ret_p", "pool_p", "conv_s", "gdn_s", "ret_s", "pool_s")}
    for l in range(depth):
        hmid = _ffn_up(hq, ss, f1w1[l], f1w3[l], 0.5)
        x, hq, ss = _proj_residual([hmid], [f1w2[l]], x, mix_norm[l], *ffn_tiles)
        pa = _in_proj(hq, ss, w_a[l])
        pab = _in_proj(hq, ss, w_ab[l])
        pb = _in_proj(hq, ss, w_b[l])
        pc = _in_proj(hq, ss, w_c[l])
        oa, sa = _gdn_mixer(pa, pab, conv0[l], gdn0[l], conv_w[l], gdn_a_log[l], gdn_dt_bias[l], gdn_norm_w[l], sched)
        ob, sb = _ret_mixer(pb, ret0[l], sched, n_pos)
        oc = _pool_mixer(pc, pool0[l], pool_w[l], pool_scale[l], sched)
        x, hq, ss = _proj_residual([oa, ob, oc], [w_oa[l], w_ob[l], w_oc[l]], x, ffn2_norm[l], *out_tiles)
        hmid = _ffn_up(hq, ss, f2w1[l], f2w3[l], 0.5)
        if l + 1 < depth:
            x, hq, ss = _proj_residual([hmid], [f2w2[l]], x, ffn1_norm[l + 1], *ffn_tiles)
        else:
            x = _proj_residual([hmid], [f2w2[l]], x, None, *ffn_tiles)
        cp, cs = _last_rows(pa, sched, conv_w.shape[1] - 1, in_sizes[0])
        pp, ps = _last_rows(pc, sched, state_pool.shape[2], pool_width)
        for k, v in (("conv_p", cp), ("conv_s", cs), ("pool_p", pp), ("pool_s", ps), ("gdn_p", sa[:bp]),
                     ("gdn_s", sa[bp:]), ("ret_p", sb[:bp]), ("ret_s", sb[bp:])):
            outs[k].append(v)

    y_prompt = _final_norm(x, final_norm, 0, mp).reshape(bp, lp, d)
    y_sample = _final_norm(x, final_norm, mp, ms).reshape(bs, ls, d)
    st = {k: jnp.stack(v) for k, v in outs.items()}
    return (y_prompt, y_sample, st["conv_p"], st["gdn_p"], st["ret_p"], st["pool_p"], st["conv_s"], st["gdn_s"],
            st["ret_s"], st["pool_s"])
```

```python
import functools
import math

import jax
import jax.numpy as jnp
from jax import lax
from jax.experimental import pallas as pl
from jax.experimental.pallas import tpu as pltpu

CHUNK = 64
EPS = 1e-6
ROPE_BASE = 10000.0
POOL_WINDOWS = (2, 4, 8, 16)
PAST_LEN = 4096
LANES = 128
FFN_TILE = 512
VMEM_LIMIT_BYTES = 56 * 1024 * 1024

f32 = jnp.float32
bf16 = jnp.bfloat16
HI = lax.Precision.HIGHEST


def _pick(n, candidates):
    for c in candidates:
        if n % c == 0:
            return c
    raise ValueError(f"no block size in {candidates} divides {n}")


def _params(sem):
    return pltpu.CompilerParams(dimension_semantics=sem, vmem_limit_bytes=VMEM_LIMIT_BYTES)


def _silu(x):
    return x * jax.nn.sigmoid(x)


def _dot(a, b):
    return jnp.dot(a, b, preferred_element_type=f32)


def _dot_hi(a, b):
    return jnp.dot(a, b, preferred_element_type=f32, precision=HI)


def _dot_nt(a, b):
    return lax.dot_general(a, b, (((1,), (1,)), ((), ())), preferred_element_type=f32)


def _dot_tn(a, b):
    return lax.dot_general(a, b, (((0,), (0,)), ((), ())), preferred_element_type=f32)


def _row_rsqrt(ss_ref, d_model):
    return lax.rsqrt(ss_ref[:, 0:1] / d_model + EPS)


def _norm_prep_kernel(xa_ref, xb_ref, g_ref, x_ref, hq_ref, ss_ref, *, na):
    def emit(src_ref):
        x = src_ref[...]
        x_ref[...] = x
        hq_ref[...] = (x * g_ref[...]).astype(hq_ref.dtype)
        ss_ref[...] = jnp.broadcast_to(jnp.sum(x * x, axis=-1, keepdims=True), ss_ref.shape)

    i = pl.program_id(0)
    pl.when(i < na)(lambda: emit(xa_ref))
    pl.when(i >= na)(lambda: emit(xb_ref))


def _norm_prep(xa, xb, g):
    (ma, d), mb = xa.shape, xb.shape[0]
    bm = _pick(math.gcd(ma, mb), (256, 128, 64))
    na, m = ma // bm, ma + mb
    row = pl.BlockSpec((bm, d), lambda i: (i, 0))
    return pl.pallas_call(
        functools.partial(_norm_prep_kernel, na=na),
        grid=(m // bm,),
        in_specs=[
            pl.BlockSpec((bm, d), lambda i: (jnp.minimum(i, na - 1), 0)),
            pl.BlockSpec((bm, d), lambda i: (jnp.maximum(i - na, 0), 0)),
            pl.BlockSpec((1, d), lambda i: (0, 0)),
        ],
        out_specs=[row, row, pl.BlockSpec((bm, LANES), lambda i: (i, 0))],
        out_shape=[jax.ShapeDtypeStruct((m, d), f32), jax.ShapeDtypeStruct((m, d), bf16),
                   jax.ShapeDtypeStruct((m, LANES), f32)],
        compiler_params=_params(("arbitrary",)),
        name="norm_prep",
    )(xa, xb, g.reshape(1, d))


def _final_norm_kernel(x_ref, g_ref, o_ref):
    x = x_ref[...]
    o_ref[...] = x * lax.rsqrt(jnp.mean(x * x, axis=-1, keepdims=True) + EPS) * g_ref[...]


def _final_norm(x, g, row0, n_rows):
    d = x.shape[1]
    bm = _pick(n_rows, (512, 256, 128, 64))
    assert row0 % bm == 0
    b0 = row0 // bm
    return pl.pallas_call(
        _final_norm_kernel,
        grid=(n_rows // bm,),
        in_specs=[pl.BlockSpec((bm, d), lambda i: (i + b0, 0)), pl.BlockSpec((1, d), lambda i: (0, 0))],
        out_specs=pl.BlockSpec((bm, d), lambda i: (i, 0)),
        out_shape=jax.ShapeDtypeStruct((n_rows, d), f32),
        compiler_params=_params(("parallel",)),
        name="final_norm",
    )(x, g.reshape(1, d))


def _up_kernel(h_ref, ss_ref, w1_ref, w3_ref, o_ref, *, d_model, scale):
    h = h_ref[...]
    r = _row_rsqrt(ss_ref, d_model)
    g = _dot(h, w1_ref[...]) * r
    u = _dot(h, w3_ref[...]) * r
    o_ref[...] = (scale * _silu(g) * u).astype(o_ref.dtype)


def _tile_cols(w, bn_cands):
    depth, k, n = w.shape
    bn = _pick(n, bn_cands)
    return w.reshape(depth, k, n // bn, bn).transpose(0, 2, 1, 3).astype(bf16)


def _layer_cols(w, layer, row0, rows):
    assert row0 % rows == 0
    return pl.BlockSpec((None, None, rows, w.shape[3]), lambda i, j: (layer, j, row0 // rows, 0))


def _ffn_up(hq, ss, w1, w3, layer, scale):
    m, d = hq.shape
    bn = w1.shape[3]
    f = w1.shape[1] * bn
    bm = _pick(m, (1024, 512, 256, 128, 64))
    return pl.pallas_call(
        functools.partial(_up_kernel, d_model=d, scale=scale),
        grid=(m // bm, f // bn),
        in_specs=[
            pl.BlockSpec((bm, d), lambda i, j: (i, 0)),
            pl.BlockSpec((bm, LANES), lambda i, j: (i, 0)),
            _layer_cols(w1, layer, 0, d),
            _layer_cols(w3, layer, 0, d),
        ],
        out_specs=pl.BlockSpec((bm, bn), lambda i, j: (i, j)),
        out_shape=jax.ShapeDtypeStruct((m, f), bf16),
        compiler_params=_params(("parallel", "arbitrary")),
        name="ffn_up",
    )(hq, ss, w1, w3)


def _in_proj_kernel(h_ref, ss_ref, w_ref, o_ref, *, d_model):
    o_ref[...] = _dot(h_ref[...], w_ref[...]) * _row_rsqrt(ss_ref, d_model)


def _in_proj(hq, ss, w, layer):
    m, d = hq.shape
    bn = w.shape[3]
    n = w.shape[1] * bn
    bm = _pick(m, (1024, 512, 256, 128, 64))
    return pl.pallas_call(
        functools.partial(_in_proj_kernel, d_model=d),
        grid=(m // bm, n // bn),
        in_specs=[
            pl.BlockSpec((bm, d), lambda i, j: (i, 0)),
            pl.BlockSpec((bm, LANES), lambda i, j: (i, 0)),
            _layer_cols(w, layer, 0, d),
        ],
        out_specs=pl.BlockSpec((bm, bn), lambda i, j: (i, j)),
        out_shape=jax.ShapeDtypeStruct((m, n), f32),
        compiler_params=_params(("parallel", "arbitrary")),
        name="in_proj",
    )(hq, ss, w)


def _proj_residual_kernel(*refs, n_in, with_norm):
    a_refs, w_refs = refs[:n_in], refs[n_in:2 * n_in]
    x_ref = refs[2 * n_in]
    acc = _dot(a_refs[0][...], w_refs[0][...])
    for a_ref, w_ref in zip(a_refs[1:], w_refs[1:]):
        acc = acc + _dot(a_ref[...], w_ref[...])
    xn = x_ref[...] + acc
    if not with_norm:
        refs[2 * n_in + 1][...] = xn
        return
    g_ref, xo_ref, hq_ref, ss_ref = refs[2 * n_in + 1:]
    xo_ref[...] = xn
    hq_ref[...] = (xn * g_ref[...]).astype(hq_ref.dtype)
    part = jnp.broadcast_to(jnp.sum(xn * xn, axis=-1, keepdims=True), ss_ref.shape)
    j = pl.program_id(1)

    @pl.when(j == 0)
    def _():
        ss_ref[...] = part

    @pl.when(j != 0)
    def _():
        ss_ref[...] += part


def _proj_residual(a_list, w, layer, x, g, bm_cands):
    m, d = x.shape
    bm = _pick(m, bm_cands)
    bn = w.shape[3]
    assert w.shape[1] * bn == d
    with_norm = g is not None
    tile = pl.BlockSpec((bm, bn), lambda i, j: (i, j))
    in_specs = [pl.BlockSpec((bm, a.shape[1]), lambda i, j: (i, 0)) for a in a_list]
    row0 = 0
    for a in a_list:
        in_specs.append(_layer_cols(w, layer, row0, a.shape[1]))
        row0 += a.shape[1]
    assert row0 == w.shape[2]
    in_specs.append(tile)
    args = list(a_list) + [w] * len(a_list) + [x]
    if with_norm:
        in_specs.append(pl.BlockSpec((1, bn), lambda i, j: (0, j)))
        args.append(g.reshape(1, d))
        out_specs = [tile, tile, pl.BlockSpec((bm, LANES), lambda i, j: (i, 0))]
        out_shape = [jax.ShapeDtypeStruct((m, d), f32), jax.ShapeDtypeStruct((m, d), bf16),
                     jax.ShapeDtypeStruct((m, LANES), f32)]
    else:
        out_specs, out_shape = tile, jax.ShapeDtypeStruct((m, d), f32)
    return pl.pallas_call(
        functools.partial(_proj_residual_kernel, n_in=len(a_list), with_norm=with_norm),
        grid=(m // bm, d // bn),
        in_specs=in_specs,
        out_specs=out_specs,
        out_shape=out_shape,
        compiler_params=_params(("parallel", "arbitrary")),
        name="proj_residual",
    )(*args)


class _Sched:
    def __init__(self, bp, lp, bs, ls):
        assert lp % CHUNK == 0 and ls % CHUNK == 0 and PAST_LEN % CHUNK == 0
        self.bp, self.np_, self.bs, self.ns = bp, lp // CHUNK, bs, ls // CHUNK
        self.n_prompt = self.bp * self.np_
        self.n_chunks = self.n_prompt + self.bs * self.ns
        self.n_seq = bp + bs

    def split(self, c):
        in_prompt = c < self.n_prompt
        cs = jnp.maximum(c - self.n_prompt, 0)
        seq = jnp.where(in_prompt, c // self.np_, self.bp + cs // self.ns)
        ci = jnp.where(in_prompt, c % self.np_, cs % self.ns)
        last = jnp.where(in_prompt, self.np_ - 1, self.ns - 1)
        return in_prompt, seq, ci, ci == last

    def seq(self, c):
        return self.split(c)[1]

    def pos_chunk(self, c):
        in_prompt, _, ci, _ = self.split(c)
        return jnp.where(in_prompt, 0, PAST_LEN // CHUNK) + ci


def _strict_lower_inverse_minus_eye(lows, row, col):
    pair = (row // 2) == (col // 2)
    es = [-jnp.where(pair, low, 0.0) for low in lows]
    b = 2
    while b < CHUNK:
        mask = ((row // (2 * b)) == (col // (2 * b))) & ((row // b) != (col // b))
        cs = [jnp.where(mask, low, 0.0) for low in lows]
        xs = [c + _dot(e.astype(bf16), c.astype(bf16)) for e, c in zip(es, cs)]
        ys = [x + _dot(x.astype(bf16), e.astype(bf16)) for x, e in zip(xs, es)]
        es = [e - y for e, y in zip(es, ys)]
        b *= 2
    return es


def _gdn_kernel(qkv_ref, z_ref, ab_ref, cinit_ref, sinit_ref, cw_ref, alog_ref, dtb_ref, nw_ref,
                o_ref, sout_ref, cout_ref, xp_ref, s_ref, gt_ref, *, sched, n_heads, dk, dv):
    c = pl.program_id(0)
    _, _, ci, is_last = sched.split(c)
    w = n_heads * dk
    hist = 8
    kw = cw_ref.shape[0]

    @pl.when(ci == 0)
    def _():
        xp_ref[0:hist, :] = jnp.zeros((hist, xp_ref.shape[1]), f32)
        xp_ref[hist - (kw - 1):hist, :] = cinit_ref[0]
        s_ref[...] = sinit_ref[0]

    @pl.when(ci != 0)
    def _():
        xp_ref[0:hist, :] = xp_ref[CHUNK:CHUNK + hist, :]

    xp_ref[hist:hist + CHUNK, :] = qkv_ref[...]

    ab = ab_ref[...]
    sp = jnp.maximum(ab + dtb_ref[...], 0.0) + jnp.log1p(jnp.exp(-jnp.abs(ab + dtb_ref[...])))
    g_all = -jnp.exp(alog_ref[...]) * sp
    beta_all = jax.nn.sigmoid(ab)

    r2 = lax.broadcasted_iota(jnp.int32, (2 * CHUNK, CHUNK), 0)
    c2 = lax.broadcasted_iota(jnp.int32, (2 * CHUNK, CHUNK), 1)
    gp = _dot_hi((c2 <= r2).astype(f32), g_all)
    gt_ref[...] = gp.T
    g_cum = gp[0:CHUNK, :]
    eg_all = jnp.exp(g_cum)
    g_last = g_cum[CHUNK - 1:CHUNK, :]
    tail_all = jnp.exp(g_last - g_cum)
    gblk_all = jnp.exp(g_last)

    row = lax.broadcasted_iota(jnp.int32, (CHUNK, CHUNK), 0)
    col = lax.broadcasted_iota(jnp.int32, (CHUNK, CHUNK), 1)
    causal = row >= col
    strict = row > col

    def conv(off):
        acc = xp_ref[hist - (kw - 1):hist - (kw - 1) + CHUNK, off:off + dk] * cw_ref[0:1, off:off + dk]
        for i in range(1, kw):
            r0 = hist - (kw - 1) + i
            acc = acc + xp_ref[r0:r0 + CHUNK, off:off + dk] * cw_ref[i:i + 1, off:off + dk]
        return _silu(acc)

    def l2n(t):
        return t * lax.rsqrt(jnp.sum(t * t, axis=-1, keepdims=True) + EPS)

    heads = range(n_heads)
    kn = [l2n(conv(w + h * dk)) for h in heads]
    kb = [t.astype(bf16) for t in kn]
    kk = [_dot_nt(t, t) for t in kb]
    qn = [l2n(conv(h * dk)) * (dk ** -0.5) for h in heads]
    qk = [_dot_nt(qn[h].astype(bf16), kb[h]) for h in heads]
    bc = [beta_all[:, n_heads + h:n_heads + h + 1] for h in heads]
    egc = [eg_all[:, h:h + 1] for h in heads]
    decay = [jnp.where(causal, jnp.exp(jnp.where(causal, g_cum[:, h:h + 1] - gt_ref[h:h + 1, 0:CHUNK], 0.0)), 0.0)
             for h in heads]
    lows = [jnp.where(strict, kk[h] * decay[h] * bc[h], 0.0) for h in heads]
    es = _strict_lower_inverse_minus_eye(lows, row, col)
    rhs = [jnp.concatenate([conv(2 * w + h * dv) * bc[h], kn[h] * (bc[h] * egc[h])], axis=-1) for h in heads]
    sol = [rhs[h] + _dot(es[h].astype(bf16), rhs[h].astype(bf16)) for h in heads]

    sb = [s_ref[h].astype(bf16) for h in heads]
    ws = [_dot(sol[h][:, dv:dv + dk].astype(bf16), sb[h]) for h in heads]
    ub = [(sol[h][:, 0:dv] - ws[h]).astype(bf16) for h in heads]
    qs = [_dot((qn[h] * egc[h]).astype(bf16), sb[h]) for h in heads]
    qu = [_dot((qk[h] * decay[h]).astype(bf16), ub[h]) for h in heads]
    ku = [_dot_tn((kn[h] * tail_all[:, h:h + 1]).astype(bf16), ub[h]) for h in heads]
    for h in heads:
        s_ref[h] = s_ref[h] * gblk_all[:, h:h + 1] + ku[h]
        o = qs[h] + qu[h]
        o = o * lax.rsqrt(jnp.mean(o * o, axis=-1, keepdims=True) + EPS) * nw_ref[...]
        o = o * _silu(z_ref[:, h * dv:(h + 1) * dv])
        o_ref[:, h * dv:(h + 1) * dv] = o.astype(o_ref.dtype)

    @pl.when(is_last)
    def _():
        sout_ref[0] = s_ref[...]
        cout_ref[0] = xp_ref[hist + CHUNK - (kw - 1):hist + CHUNK, :]


def _gdn_mixer(pa, pab, conv_init, s_init, layer, conv_w, a_log, dt_bias, norm_w, sched):
    m = pa.shape[0]
    _, n_seq, n_heads, dk, dv = s_init.shape
    assert dk == dv == LANES and 2 * n_heads <= LANES
    w = n_heads * dk
    kw = conv_w.shape[0]
    pad = lambda t: jnp.pad(t.reshape(1, -1), ((0, 0), (0, LANES - t.size)))
    return pl.pallas_call(
        functools.partial(_gdn_kernel, sched=sched, n_heads=n_heads, dk=dk, dv=dv),
        grid=(sched.n_chunks,),
        in_specs=[
            pl.BlockSpec((CHUNK, 3 * w), lambda c: (c, 0)),
            pl.BlockSpec((CHUNK, w), lambda c: (c, 3)),
            pl.BlockSpec((CHUNK, LANES), lambda c: (c, 0)),
            pl.BlockSpec((None, 1, kw - 1, 3 * w), lambda c: (layer, sched.seq(c), 0, 0)),
            pl.BlockSpec((None, 1, n_heads, dk, dv), lambda c: (layer, sched.seq(c), 0, 0, 0)),
            pl.BlockSpec((kw, 3 * w), lambda c: (0, 0)),
            pl.BlockSpec((1, LANES), lambda c: (0, 0)),
            pl.BlockSpec((1, LANES), lambda c: (0, 0)),
            pl.BlockSpec((1, dv), lambda c: (0, 0)),
        ],
        out_specs=[
            pl.BlockSpec((CHUNK, w), lambda c: (c, 0)),
            pl.BlockSpec((1, n_heads, dk, dv), lambda c: (sched.seq(c), 0, 0, 0)),
            pl.BlockSpec((1, kw - 1, 3 * w), lambda c: (sched.seq(c), 0, 0)),
        ],
        out_shape=[jax.ShapeDtypeStruct((m, w), bf16), jax.ShapeDtypeStruct(s_init.shape[1:], f32),
                   jax.ShapeDtypeStruct((n_seq, kw - 1, 3 * w), f32)],
        scratch_shapes=[
            pltpu.VMEM((CHUNK + 8, 3 * w), f32),
            pltpu.VMEM((n_heads, dk, dv), f32),
            pltpu.VMEM((LANES, 2 * CHUNK), f32),
        ],
        compiler_params=_params(("arbitrary",)),
        name="gdn_mixer",
    )(pa, pa, pab, conv_init, s_init, conv_w, pad(a_log), pad(dt_bias), norm_w.reshape(1, dv))


def _retpool_kernel(q_ref, k_ref, v_ref, g_ref, cos_ref, sin_ref, dmat_ref, qdec_ref, ktail_ref, gblk_ref, sinit_ref,
                    u_ref, binit_ref, pw_ref, pscale_ref, o_ref, sout_ref, oc_ref, bout_ref, s_ref, up_ref,
                    *, sched, n_heads, dk, dv, gc):
    c = pl.program_id(0)
    in_prompt, _, ci, is_last = sched.split(c)
    hist = 16
    nbuf = binit_ref.shape[1]

    @pl.when(ci == 0)
    def _():
        s_ref[...] = sinit_ref[0]
        up_ref[0:hist, :] = jnp.zeros((hist, up_ref.shape[1]), f32)
        up_ref[hist - nbuf:hist, :] = binit_ref[0]

    @pl.when(ci != 0)
    def _():
        up_ref[0:hist, :] = up_ref[CHUNK:CHUNK + hist, :]

    up_ref[hist:hist + CHUNK, :] = u_ref[...]

    cos = cos_ref[...]
    sin = sin_ref[...]

    def rope(t):
        return t * cos + pltpu.roll(t, dk // 2, axis=1) * sin

    heads = range(n_heads)
    qr = [rope(q_ref[:, h * dk:(h + 1) * dk]) for h in heads]
    kr = [rope(k_ref[:, h * dk:(h + 1) * dk]) * (dk ** -0.5) for h in heads]
    vb = [v_ref[:, h * dv:(h + 1) * dv].astype(bf16) for h in heads]
    scores = [_dot_nt(qr[h].astype(bf16), kr[h].astype(bf16)) * dmat_ref[h] for h in heads]
    cross = [_dot((qr[h] * qdec_ref[h]).astype(bf16), s_ref[h].astype(bf16)) for h in heads]
    inner = [_dot(scores[h].astype(bf16), vb[h]) for h in heads]
    kv = [_dot_tn((kr[h] * ktail_ref[h]).astype(bf16), vb[h]) for h in heads]
    for h in heads:
        s_ref[h] = s_ref[h] * gblk_ref[h] + kv[h]
        o = inner[h] + cross[h]
        oc = o - jnp.mean(o, axis=-1, keepdims=True)
        on = oc * lax.rsqrt(jnp.mean(oc * oc, axis=-1, keepdims=True) + EPS)
        o_ref[:, h * dv:(h + 1) * dv] = (on * _silu(g_ref[:, h * dv:(h + 1) * dv])).astype(o_ref.dtype)

    pos = (jnp.where(in_prompt, 0, PAST_LEN) + ci * CHUNK
           + lax.broadcasted_iota(jnp.int32, (CHUNK, 1), 0))
    for gi, win_len in enumerate(POOL_WINDOWS):
        cols = slice(gi * gc, (gi + 1) * gc)
        cur = up_ref[hist:hist + CHUNK, cols]
        win = cur
        for j in range(1, win_len):
            win = win + up_ref[hist - j:hist - j + CHUNK, cols]
        cnt = jnp.minimum(pos + 1, win_len).astype(f32)
        d = win / cnt - cur
        y = _dot(d.astype(bf16), pw_ref[gi]) * pscale_ref[:, cols]
        oc_ref[:, cols] = y.astype(oc_ref.dtype)

    @pl.when(is_last)
    def _():
        sout_ref[0] = s_ref[...]
        bout_ref[0] = up_ref[hist + CHUNK - nbuf:hist + CHUNK, :]


def _ret_tables(n_heads, dk, dv, n_pos):
    half = dk // 2
    inv_freq = ROPE_BASE ** (-jnp.arange(half, dtype=f32) / half)
    ang = jnp.arange(n_pos, dtype=jnp.int32).astype(f32)[:, None] * inv_freq[None, :]
    cos, sin = jnp.cos(ang), jnp.sin(ang)
    cos2 = jnp.concatenate([cos, cos], axis=-1)
    sin2 = jnp.concatenate([-sin, sin], axis=-1)
    lg = jnp.log1p(-jnp.exp2(-5.0 - jnp.arange(n_heads, dtype=f32)))
    idx = jnp.arange(CHUNK, dtype=f32)
    rel = idx[:, None] - idx[None, :]
    causal = rel >= 0
    dmat = jnp.where(causal, jnp.exp(jnp.where(causal, rel, 0.0) * lg[:, None, None]), 0.0)
    qdec = jnp.exp((idx + 1.0)[None, :] * lg[:, None])
    ktail = jnp.exp((CHUNK - 1.0 - idx)[None, :] * lg[:, None])
    gblk = jnp.exp(CHUNK * lg)
    bl = lambda t: jnp.broadcast_to(t[:, :, None], (n_heads, CHUNK, dk))
    return cos2, sin2, dmat, bl(qdec), bl(ktail), jnp.broadcast_to(gblk[:, None, None], (n_heads, 1, dv))


def _retpool_mixer(pb, pc, s_init, buf_init, layer, pool_w, pool_scale, sched, n_pos):
    m, width = pc.shape
    _, n_seq, n_heads, dk, dv = s_init.shape
    n_grp, gc, _ = pool_w.shape
    nbuf = buf_init.shape[2]
    assert dk == LANES and dv == 2 * dk
    assert n_grp == len(POOL_WINDOWS) and nbuf == max(POOL_WINDOWS) - 1 and gc % LANES == 0
    qw, vw = n_heads * dk, n_heads * dv
    cos2, sin2, dmat, qdec, ktail, gblk = _ret_tables(n_heads, dk, dv, n_pos)
    whole = lambda a: pl.BlockSpec(a.shape, lambda c: (0,) * a.ndim)
    return pl.pallas_call(
        functools.partial(_retpool_kernel, sched=sched, n_heads=n_heads, dk=dk, dv=dv, gc=gc),
        grid=(sched.n_chunks,),
        in_specs=[
            pl.BlockSpec((CHUNK, qw), lambda c: (c, 0)),
            pl.BlockSpec((CHUNK, qw), lambda c: (c, 1)),
            pl.BlockSpec((CHUNK, vw), lambda c: (c, 1)),
            pl.BlockSpec((CHUNK, vw), lambda c: (c, 2)),
            pl.BlockSpec((CHUNK, dk), lambda c: (sched.pos_chunk(c), 0)),
            pl.BlockSpec((CHUNK, dk), lambda c: (sched.pos_chunk(c), 0)),
            whole(dmat), whole(qdec), whole(ktail), whole(gblk),
            pl.BlockSpec((None, 1, n_heads, dk, dv), lambda c: (layer, sched.seq(c), 0, 0, 0)),
            pl.BlockSpec((CHUNK, width), lambda c: (c, 0)),
            pl.BlockSpec((None, 1, nbuf, width), lambda c: (layer, sched.seq(c), 0, 0)),
            pl.BlockSpec((n_grp, gc, gc), lambda c: (0, 0, 0)),
            pl.BlockSpec((1, width), lambda c: (0, 0)),
        ],
        out_specs=[
            pl.BlockSpec((CHUNK, vw), lambda c: (c, 0)),
            pl.BlockSpec((1, n_heads, dk, dv), lambda c: (sched.seq(c), 0, 0, 0)),
            pl.BlockSpec((CHUNK, width), lambda c: (c, 0)),
            pl.BlockSpec((1, nbuf, width), lambda c: (sched.seq(c), 0, 0)),
        ],
        out_shape=[jax.ShapeDtypeStruct((m, vw), bf16), jax.ShapeDtypeStruct(s_init.shape[1:], f32),
                   jax.ShapeDtypeStruct((m, width), bf16), jax.ShapeDtypeStruct((n_seq, nbuf, width), f32)],
        scratch_shapes=[pltpu.VMEM((n_heads, dk, dv), f32), pltpu.VMEM((CHUNK + 16, width), f32)],
        compiler_params=_params(("arbitrary",)),
        name="retpool_mixer",
    )(pb, pb, pb, pb, cos2, sin2, dmat, qdec, ktail, gblk, s_init, pc, buf_init, pool_w.astype(bf16),
      pool_scale.reshape(1, width))


def kernel(x_prompt, x_sample, state_conv, state_gdn, state_ret, state_pool, ffn1_norm, ffn1_w1, ffn1_w3, ffn1_w2,
           mix_norm, w_in, conv_w, gdn_a_log, gdn_dt_bias, gdn_norm_w, pool_w, pool_scale, w_out, ffn2_norm,
           ffn2_w1, ffn2_w3, ffn2_w2, final_norm):
    bp, lp, d = x_prompt.shape
    bs, ls, _ = x_sample.shape
    depth = ffn1_w1.shape[0]
    sched = _Sched(bp, lp, bs, ls)
    mp, ms = bp * lp, bs * ls
    n_pos = max(lp, PAST_LEN + ls)

    _, _, ha, dka, dva = state_gdn.shape
    _, _, hb, dkb, dvb = state_ret.shape
    wa, qkb, wb = ha * dva, hb * dkb, hb * dvb
    pool_width = state_pool.shape[-1]
    in_sizes = (2 * ha * dka + wa, wa, ha, ha, qkb, qkb, wb, wb, pool_width)
    assert sum(in_sizes) == w_in.shape[-1]
    offs = [0]
    for s in in_sizes:
        offs.append(offs[-1] + s)
    cols = lambda t, i, j: t[..., offs[i]:offs[j]]

    fpad = -ffn1_w1.shape[-1] % FFN_TILE
    res_tiles = (512, 256, 128)
    in_tiles = (1024, 768, 512, 384, 256, 128)
    cast_up = lambda t: _tile_cols(jnp.pad(t, ((0, 0), (0, 0), (0, fpad))), (FFN_TILE,))
    cast_dn = lambda t: _tile_cols(jnp.pad(t, ((0, 0), (0, fpad), (0, 0))), res_tiles)
    f1w1, f1w3, f1w2 = cast_up(ffn1_w1), cast_up(ffn1_w3), cast_dn(ffn1_w2)
    f2w1, f2w3, f2w2 = cast_up(ffn2_w1), cast_up(ffn2_w3), cast_dn(ffn2_w2)
    w_a = _tile_cols(cols(w_in, 0, 2), in_tiles)
    w_ab = _tile_cols(jnp.pad(cols(w_in, 2, 4), ((0, 0), (0, 0), (0, LANES - 2 * ha))), in_tiles)
    w_b = _tile_cols(cols(w_in, 4, 8), in_tiles)
    w_c = _tile_cols(cols(w_in, 8, 9), in_tiles)
    w_o = _tile_cols(w_out, res_tiles)

    def stack_state(st):
        return jnp.concatenate([jnp.zeros((st.shape[0], bp) + st.shape[2:], st.dtype), st], axis=1)

    conv0, gdn0, ret0, pool0 = map(stack_state, (state_conv, state_gdn, state_ret, state_pool))

    ffn_rows = (512, 256, 128, 64)
    out_rows = (1024, 512, 256, 128, 64)

    x, hq, ss = _norm_prep(x_prompt.reshape(mp, d), x_sample.reshape(ms, d), ffn1_norm[0])
    outs = {k: [] for k in ("conv_p", "gdn_p", "ret_p", "pool_p", "conv_s", "gdn_s", "ret_s", "pool_s")}
    for l in range(depth):
        hmid = _ffn_up(hq, ss, f1w1, f1w3, l, 0.5)
        x, hq, ss = _proj_residual([hmid], f1w2, l, x, mix_norm[l], ffn_rows)
        pa = _in_proj(hq, ss, w_a, l)
        pab = _in_proj(hq, ss, w_ab, l)
        pb = _in_proj(hq, ss, w_b, l)
        pc = _in_proj(hq, ss, w_c, l)
        oa, sa, ca = _gdn_mixer(pa, pab, conv0, gdn0, l, conv_w[l], gdn_a_log[l], gdn_dt_bias[l], gdn_norm_w[l],
                                sched)
        ob, sb, oc, pbuf = _retpool_mixer(pb, pc, ret0, pool0, l, pool_w[l], pool_scale[l], sched, n_pos)
        x, hq, ss = _proj_residual([oa, ob, oc], w_o, l, x, ffn2_norm[l], out_rows)
        hmid = _ffn_up(hq, ss, f2w1, f2w3, l, 0.5)
        if l + 1 < depth:
            x, hq, ss = _proj_residual([hmid], f2w2, l, x, ffn1_norm[l + 1], ffn_rows)
        else:
            x = _proj_residual([hmid], f2w2, l, x, None, ffn_rows)
        for k, v in (("conv", ca), ("pool", pbuf), ("gdn", sa), ("ret", sb)):
            outs[k + "_p"].append(v[:bp])
            outs[k + "_s"].append(v[bp:])

    y_prompt = _final_norm(x, final_norm, 0, mp).reshape(bp, lp, d)
    y_sample = _final_norm(x, final_norm, mp, ms).reshape(bs, ls, d)
    st = {k: jnp.stack(v) for k, v in outs.items()}
    return (y_prompt, y_sample, st["conv_p"], st["gdn_p"], st["ret_p"], st["pool_p"], st["conv_s"], st["gdn_s"],
            st["ret_s"], st["pool_s"])
```

```python
import functools
import math
from typing import NamedTuple

import jax
import jax.numpy as jnp
from jax import lax
from jax.experimental import pallas as pl
from jax.experimental.pallas import tpu as pltpu

CHUNK = 64
EPS = 1e-6
ROPE_BASE = 10000.0
POOL_WINDOWS = (2, 4, 8, 16)
PAST_LEN = 4096
LANES = 128
FFN_TILE = 512
VMEM_LIMIT_BYTES = 56 * 1024 * 1024

f32 = jnp.float32
bf16 = jnp.bfloat16
HI = lax.Precision.HIGHEST


def _pick(n, candidates):
    for c in candidates:
        if n % c == 0:
            return c
    raise ValueError(f"no block size in {candidates} divides {n}")


def _params(sem):
    return pltpu.CompilerParams(dimension_semantics=sem, vmem_limit_bytes=VMEM_LIMIT_BYTES)


def _silu(x):
    return x * jax.nn.sigmoid(x)


def _dot(a, b):
    return jnp.dot(a, b, preferred_element_type=f32)


def _dot_hi(a, b):
    return jnp.dot(a, b, preferred_element_type=f32, precision=HI)


def _dot_nt(a, b):
    return lax.dot_general(a, b, (((1,), (1,)), ((), ())), preferred_element_type=f32)


def _dot_tn(a, b):
    return lax.dot_general(a, b, (((0,), (0,)), ((), ())), preferred_element_type=f32)


def _row_rsqrt(ss_ref, d_model):
    return lax.rsqrt(ss_ref[:, 0:1] / d_model + EPS)


def _norm_prep_kernel(xa_ref, xb_ref, g_ref, x_ref, hq_ref, ss_ref, *, na):
    def emit(src_ref):
        x = src_ref[...]
        x_ref[...] = x
        hq_ref[...] = (x * g_ref[...]).astype(hq_ref.dtype)
        ss_ref[...] = jnp.broadcast_to(jnp.sum(x * x, axis=-1, keepdims=True), ss_ref.shape)

    i = pl.program_id(0)
    pl.when(i < na)(lambda: emit(xa_ref))
    pl.when(i >= na)(lambda: emit(xb_ref))


def _norm_prep(xa, xb, g):
    (ma, d), mb = xa.shape, xb.shape[0]
    bm = _pick(math.gcd(ma, mb), (256, 128, 64))
    na, m = ma // bm, ma + mb
    row = pl.BlockSpec((bm, d), lambda i: (i, 0))
    return pl.pallas_call(
        functools.partial(_norm_prep_kernel, na=na),
        grid=(m // bm,),
        in_specs=[
            pl.BlockSpec((bm, d), lambda i: (jnp.minimum(i, na - 1), 0)),
            pl.BlockSpec((bm, d), lambda i: (jnp.maximum(i - na, 0), 0)),
            pl.BlockSpec((1, d), lambda i: (0, 0)),
        ],
        out_specs=[row, row, pl.BlockSpec((bm, LANES), lambda i: (i, 0))],
        out_shape=[jax.ShapeDtypeStruct((m, d), f32), jax.ShapeDtypeStruct((m, d), bf16),
                   jax.ShapeDtypeStruct((m, LANES), f32)],
        compiler_params=_params(("arbitrary",)),
        name="norm_prep",
    )(xa, xb, g.reshape(1, d))


def _final_norm_kernel(x_ref, g_ref, o_ref):
    x = x_ref[...]
    o_ref[...] = x * lax.rsqrt(jnp.mean(x * x, axis=-1, keepdims=True) + EPS) * g_ref[...]


def _final_norm(x, g, row0, n_rows):
    d = x.shape[1]
    bm = _pick(n_rows, (512, 256, 128, 64))
    assert row0 % bm == 0
    b0 = row0 // bm
    return pl.pallas_call(
        _final_norm_kernel,
        grid=(n_rows // bm,),
        in_specs=[pl.BlockSpec((bm, d), lambda i: (i + b0, 0)), pl.BlockSpec((1, d), lambda i: (0, 0))],
        out_specs=pl.BlockSpec((bm, d), lambda i: (i, 0)),
        out_shape=jax.ShapeDtypeStruct((n_rows, d), f32),
        compiler_params=_params(("parallel",)),
        name="final_norm",
    )(x, g.reshape(1, d))


class _CastJob(NamedTuple):
    src: jax.Array
    layer: int
    rows_out: int
    cols_out: int


def _job_tile_rows(job, n_steps):
    rows = job.src.shape[1]
    for tr in (16, 32, 64, 128, 256, 512, 1024):
        if rows % tr == 0 and job.rows_out % tr == 0 and job.rows_out // tr <= n_steps:
            return tr
    raise ValueError("cast job does not fit the host grid")


def _up_kernel(*refs, d_model, scale, job_dims, nj):
    n_jobs = len(job_dims)
    h_ref, ss_ref, w1_ref, w3_ref = refs[:4]
    src_refs, o_ref, dst_refs = refs[4:4 + n_jobs], refs[4 + n_jobs], refs[5 + n_jobs:]
    h = h_ref[...]
    r = _row_rsqrt(ss_ref, d_model)
    g = _dot(h, w1_ref[...]) * r
    u = _dot(h, w3_ref[...]) * r
    o_ref[...] = (scale * _silu(g) * u).astype(o_ref.dtype)

    t = pl.program_id(0) * nj + pl.program_id(1)
    for (tr, rows, cols, rows_out, cols_out), src_ref, dst_ref in zip(job_dims, src_refs, dst_refs):
        val = src_ref[...].astype(bf16)
        if rows_out > rows:
            val = jnp.where(t < rows // tr, val, jnp.zeros_like(val))
        dst_ref[:, 0:cols] = val
        if cols_out > cols:
            dst_ref[:, cols:cols_out] = jnp.zeros((tr, cols_out - cols), bf16)


def _wspec(w, layer, row0, rows, bn):
    assert row0 % rows == 0
    if w.ndim == 2:
        return pl.BlockSpec((rows, bn), lambda i, j: (row0 // rows, j))
    return pl.BlockSpec((None, rows, bn), lambda i, j: (layer, row0 // rows, j))


def _ffn_up(hq, ss, w1, w3, layer, scale, jobs=()):
    m, d = hq.shape
    f = w1.shape[-1]
    bm = _pick(m, (1024, 512, 256, 128, 64))
    bn = _pick(f, (FFN_TILE, 256, 128))
    ni, nj = m // bm, f // bn
    tile_rows = tuple(_job_tile_rows(job, ni * nj) for job in jobs)
    step = lambda i, j: i * nj + j
    in_specs = [
        pl.BlockSpec((bm, d), lambda i, j: (i, 0)),
        pl.BlockSpec((bm, LANES), lambda i, j: (i, 0)),
        _wspec(w1, layer, 0, d, bn),
        _wspec(w3, layer, 0, d, bn),
    ]
    out_specs = [pl.BlockSpec((bm, bn), lambda i, j: (i, j))]
    out_shape = [jax.ShapeDtypeStruct((m, f), bf16)]
    for job, tr in zip(jobs, tile_rows):
        n_src, n_all = job.src.shape[1] // tr, job.rows_out // tr
        in_specs.append(pl.BlockSpec(
            (None, tr, job.src.shape[2]),
            lambda i, j, n_src=n_src, layer=job.layer: (layer, jnp.minimum(step(i, j), n_src - 1), 0)))
        out_specs.append(pl.BlockSpec(
            (tr, job.cols_out), lambda i, j, n_all=n_all: (jnp.minimum(step(i, j), n_all - 1), 0)))
        out_shape.append(jax.ShapeDtypeStruct((job.rows_out, job.cols_out), bf16))
    job_dims = tuple((tr, job.src.shape[1], job.src.shape[2], job.rows_out, job.cols_out)
                     for job, tr in zip(jobs, tile_rows))
    res = pl.pallas_call(
        functools.partial(_up_kernel, d_model=d, scale=scale, job_dims=job_dims, nj=nj),
        grid=(ni, nj),
        in_specs=in_specs,
        out_specs=out_specs,
        out_shape=out_shape,
        compiler_params=_params(("arbitrary", "arbitrary")),
        name="ffn_up",
    )(hq, ss, w1, w3, *[job.src for job in jobs])
    return res[0], list(res[1:])


def _in_proj_kernel(h_ref, ss_ref, w_ref, o_ref, *, d_model):
    o_ref[...] = _dot(h_ref[...], w_ref[...]) * _row_rsqrt(ss_ref, d_model)


def _in_proj(hq, ss, w, layer):
    m, d = hq.shape
    n = w.shape[-1]
    bm = _pick(m, (1024, 512, 256, 128, 64))
    bn = _pick(n, (1024, 768, 512, 384, 256, 128))
    return pl.pallas_call(
        functools.partial(_in_proj_kernel, d_model=d),
        grid=(m // bm, n // bn),
        in_specs=[
            pl.BlockSpec((bm, d), lambda i, j: (i, 0)),
            pl.BlockSpec((bm, LANES), lambda i, j: (i, 0)),
            _wspec(w, layer, 0, d, bn),
        ],
        out_specs=pl.BlockSpec((bm, bn), lambda i, j: (i, j)),
        out_shape=jax.ShapeDtypeStruct((m, n), f32),
        compiler_params=_params(("parallel", "arbitrary")),
        name="in_proj",
    )(hq, ss, w)


def _proj_residual_kernel(*refs, n_in, with_norm):
    a_refs, w_refs = refs[:n_in], refs[n_in:2 * n_in]
    x_ref = refs[2 * n_in]
    acc = _dot(a_refs[0][...], w_refs[0][...])
    for a_ref, w_ref in zip(a_refs[1:], w_refs[1:]):
        acc = acc + _dot(a_ref[...], w_ref[...])
    xn = x_ref[...] + acc
    if not with_norm:
        refs[2 * n_in + 1][...] = xn
        return
    g_ref, xo_ref, hq_ref, ss_ref = refs[2 * n_in + 1:]
    xo_ref[...] = xn
    hq_ref[...] = (xn * g_ref[...]).astype(hq_ref.dtype)
    part = jnp.broadcast_to(jnp.sum(xn * xn, axis=-1, keepdims=True), ss_ref.shape)
    j = pl.program_id(1)

    @pl.when(j == 0)
    def _():
        ss_ref[...] = part

    @pl.when(j != 0)
    def _():
        ss_ref[...] += part


def _proj_residual(a_list, w, layer, x, g, bm_cands):
    m, d = x.shape
    bm = _pick(m, bm_cands)
    bn = _pick(d, (512, 256, 128))
    assert w.shape[-1] == d
    with_norm = g is not None
    tile = pl.BlockSpec((bm, bn), lambda i, j: (i, j))
    in_specs = [pl.BlockSpec((bm, a.shape[1]), lambda i, j: (i, 0)) for a in a_list]
    row0 = 0
    for a in a_list:
        in_specs.append(_wspec(w, layer, row0, a.shape[1], bn))
        row0 += a.shape[1]
    assert row0 == w.shape[-2]
    in_specs.append(tile)
    args = list(a_list) + [w] * len(a_list) + [x]
    if with_norm:
        in_specs.append(pl.BlockSpec((1, bn), lambda i, j: (0, j)))
        args.append(g.reshape(1, d))
        out_specs = [tile, tile, pl.BlockSpec((bm, LANES), lambda i, j: (i, 0))]
        out_shape = [jax.ShapeDtypeStruct((m, d), f32), jax.ShapeDtypeStruct((m, d), bf16),
                     jax.ShapeDtypeStruct((m, LANES), f32)]
    else:
        out_specs, out_shape = tile, jax.ShapeDtypeStruct((m, d), f32)
    return pl.pallas_call(
        functools.partial(_proj_residual_kernel, n_in=len(a_list), with_norm=with_norm),
        grid=(m // bm, d // bn),
        in_specs=in_specs,
        out_specs=out_specs,
        out_shape=out_shape,
        compiler_params=_params(("parallel", "arbitrary")),
        name="proj_residual",
    )(*args)


class _Sched:
    def __init__(self, bp, lp, bs, ls):
        assert lp % CHUNK == 0 and ls % CHUNK == 0 and PAST_LEN % CHUNK == 0
        self.bp, self.np_, self.bs, self.ns = bp, lp // CHUNK, bs, ls // CHUNK
        self.n_prompt = self.bp * self.np_
        self.n_chunks = self.n_prompt + self.bs * self.ns
        self.n_seq = bp + bs

    def split(self, c):
        in_prompt = c < self.n_prompt
        cs = jnp.maximum(c - self.n_prompt, 0)
        seq = jnp.where(in_prompt, c // self.np_, self.bp + cs // self.ns)
        ci = jnp.where(in_prompt, c % self.np_, cs % self.ns)
        last = jnp.where(in_prompt, self.np_ - 1, self.ns - 1)
        return in_prompt, seq, ci, ci == last

    def seq(self, c):
        return self.split(c)[1]

    def pos_chunk(self, c):
        in_prompt, _, ci, _ = self.split(c)
        return jnp.where(in_prompt, 0, PAST_LEN // CHUNK) + ci


def _strict_lower_inverse_minus_eye(lows, row, col):
    pair = (row // 2) == (col // 2)
    es = [-jnp.where(pair, low, 0.0) for low in lows]
    b = 2
    while b < CHUNK:
        mask = ((row // (2 * b)) == (col // (2 * b))) & ((row // b) != (col // b))
        cs = [jnp.where(mask, low, 0.0) for low in lows]
        xs = [c + _dot(e.astype(bf16), c.astype(bf16)) for e, c in zip(es, cs)]
        ys = [x + _dot(x.astype(bf16), e.astype(bf16)) for x, e in zip(xs, es)]
        es = [e - y for e, y in zip(es, ys)]
        b *= 2
    return es


def _gdn_kernel(qkv_ref, z_ref, ab_ref, cinit_ref, sinit_ref, cw_ref, alog_ref, dtb_ref, nw_ref,
                o_ref, sout_ref, cout_ref, xp_ref, s_ref, gt_ref, *, sched, n_heads, dk, dv):
    c = pl.program_id(0)
    _, _, ci, is_last = sched.split(c)
    w = n_heads * dk
    hist = 8
    kw = cw_ref.shape[0]

    @pl.when(ci == 0)
    def _():
        xp_ref[0:hist, :] = jnp.zeros((hist, xp_ref.shape[1]), f32)
        xp_ref[hist - (kw - 1):hist, :] = cinit_ref[0]
        s_ref[...] = sinit_ref[0]

    @pl.when(ci != 0)
    def _():
        xp_ref[0:hist, :] = xp_ref[CHUNK:CHUNK + hist, :]

    xp_ref[hist:hist + CHUNK, :] = qkv_ref[...]

    ab = ab_ref[...]
    sp = jnp.maximum(ab + dtb_ref[...], 0.0) + jnp.log1p(jnp.exp(-jnp.abs(ab + dtb_ref[...])))
    g_all = -jnp.exp(alog_ref[...]) * sp
    beta_all = jax.nn.sigmoid(ab)

    r2 = lax.broadcasted_iota(jnp.int32, (2 * CHUNK, CHUNK), 0)
    c2 = lax.broadcasted_iota(jnp.int32, (2 * CHUNK, CHUNK), 1)
    gp = _dot_hi((c2 <= r2).astype(f32), g_all)
    gt_ref[...] = gp.T
    g_cum = gp[0:CHUNK, :]
    eg_all = jnp.exp(g_cum)
    g_last = g_cum[CHUNK - 1:CHUNK, :]
    tail_all = jnp.exp(g_last - g_cum)
    gblk_all = jnp.exp(g_last)

    row = lax.broadcasted_iota(jnp.int32, (CHUNK, CHUNK), 0)
    col = lax.broadcasted_iota(jnp.int32, (CHUNK, CHUNK), 1)
    causal = row >= col
    strict = row > col

    def conv(off):
        acc = xp_ref[hist - (kw - 1):hist - (kw - 1) + CHUNK, off:off + dk] * cw_ref[0:1, off:off + dk]
        for i in range(1, kw):
            r0 = hist - (kw - 1) + i
            acc = acc + xp_ref[r0:r0 + CHUNK, off:off + dk] * cw_ref[i:i + 1, off:off + dk]
        return _silu(acc)

    def l2n(t):
        return t * lax.rsqrt(jnp.sum(t * t, axis=-1, keepdims=True) + EPS)

    heads = range(n_heads)
    kn = [l2n(conv(w + h * dk)) for h in heads]
    kb = [t.astype(bf16) for t in kn]
    kk = [_dot_nt(t, t) for t in kb]
    qn = [l2n(conv(h * dk)) * (dk ** -0.5) for h in heads]
    qk = [_dot_nt(qn[h].astype(bf16), kb[h]) for h in heads]
    bc = [beta_all[:, n_heads + h:n_heads + h + 1] for h in heads]
    egc = [eg_all[:, h:h + 1] for h in heads]
    decay = [jnp.where(causal, jnp.exp(jnp.where(causal, g_cum[:, h:h + 1] - gt_ref[h:h + 1, 0:CHUNK], 0.0)), 0.0)
             for h in heads]
    lows = [jnp.where(strict, kk[h] * decay[h] * bc[h], 0.0) for h in heads]
    es = _strict_lower_inverse_minus_eye(lows, row, col)
    rhs = [jnp.concatenate([conv(2 * w + h * dv) * bc[h], kn[h] * (bc[h] * egc[h])], axis=-1) for h in heads]
    sol = [rhs[h] + _dot(es[h].astype(bf16), rhs[h].astype(bf16)) for h in heads]

    sb = [s_ref[h].astype(bf16) for h in heads]
    ws = [_dot(sol[h][:, dv:dv + dk].astype(bf16), sb[h]) for h in heads]
    ub = [(sol[h][:, 0:dv] - ws[h]).astype(bf16) for h in heads]
    qs = [_dot((qn[h] * egc[h]).astype(bf16), sb[h]) for h in heads]
    qu = [_dot((qk[h] * decay[h]).astype(bf16), ub[h]) for h in heads]
    ku = [_dot_tn((kn[h] * tail_all[:, h:h + 1]).astype(bf16), ub[h]) for h in heads]
    for h in heads:
        s_ref[h] = s_ref[h] * gblk_all[:, h:h + 1] + ku[h]
        o = qs[h] + qu[h]
        o = o * lax.rsqrt(jnp.mean(o * o, axis=-1, keepdims=True) + EPS) * nw_ref[...]
        o = o * _silu(z_ref[:, h * dv:(h + 1) * dv])
        o_ref[:, h * dv:(h + 1) * dv] = o.astype(o_ref.dtype)

    @pl.when(is_last)
    def _():
        sout_ref[0] = s_ref[...]
        cout_ref[0] = xp_ref[hist + CHUNK - (kw - 1):hist + CHUNK, :]


def _gdn_mixer(pa, pab, conv_init, s_init, layer, conv_w, a_log, dt_bias, norm_w, sched):
    m = pa.shape[0]
    _, n_seq, n_heads, dk, dv = s_init.shape
    assert dk == dv == LANES and 2 * n_heads <= LANES
    w = n_heads * dk
    kw = conv_w.shape[0]
    pad = lambda t: jnp.pad(t.reshape(1, -1), ((0, 0), (0, LANES - t.size)))
    return pl.pallas_call(
        functools.partial(_gdn_kernel, sched=sched, n_heads=n_heads, dk=dk, dv=dv),
        grid=(sched.n_chunks,),
        in_specs=[
            pl.BlockSpec((CHUNK, 3 * w), lambda c: (c, 0)),
            pl.BlockSpec((CHUNK, w), lambda c: (c, 3)),
            pl.BlockSpec((CHUNK, LANES), lambda c: (c, 0)),
            pl.BlockSpec((None, 1, kw - 1, 3 * w), lambda c: (layer, sched.seq(c), 0, 0)),
            pl.BlockSpec((None, 1, n_heads, dk, dv), lambda c: (layer, sched.seq(c), 0, 0, 0)),
            pl.BlockSpec((kw, 3 * w), lambda c: (0, 0)),
            pl.BlockSpec((1, LANES), lambda c: (0, 0)),
            pl.BlockSpec((1, LANES), lambda c: (0, 0)),
            pl.BlockSpec((1, dv), lambda c: (0, 0)),
        ],
        out_specs=[
            pl.BlockSpec((CHUNK, w), lambda c: (c, 0)),
            pl.BlockSpec((1, n_heads, dk, dv), lambda c: (sched.seq(c), 0, 0, 0)),
            pl.BlockSpec((1, kw - 1, 3 * w), lambda c: (sched.seq(c), 0, 0)),
        ],
        out_shape=[jax.ShapeDtypeStruct((m, w), bf16), jax.ShapeDtypeStruct(s_init.shape[1:], f32),
                   jax.ShapeDtypeStruct((n_seq, kw - 1, 3 * w), f32)],
        scratch_shapes=[
            pltpu.VMEM((CHUNK + 8, 3 * w), f32),
            pltpu.VMEM((n_heads, dk, dv), f32),
            pltpu.VMEM((LANES, 2 * CHUNK), f32),
        ],
        compiler_params=_params(("arbitrary",)),
        name="gdn_mixer",
    )(pa, pa, pab, conv_init, s_init, conv_w, pad(a_log), pad(dt_bias), norm_w.reshape(1, dv))


def _retpool_kernel(q_ref, k_ref, v_ref, g_ref, cos_ref, sin_ref, dmat_ref, qdec_ref, ktail_ref, gblk_ref, sinit_ref,
                    u_ref, binit_ref, pw_ref, pscale_ref, o_ref, sout_ref, oc_ref, bout_ref, s_ref, up_ref,
                    *, sched, n_heads, dk, dv, gc):
    c = pl.program_id(0)
    in_prompt, _, ci, is_last = sched.split(c)
    hist = 16
    nbuf = binit_ref.shape[1]

    @pl.when(ci == 0)
    def _():
        s_ref[...] = sinit_ref[0]
        up_ref[0:hist, :] = jnp.zeros((hist, up_ref.shape[1]), f32)
        up_ref[hist - nbuf:hist, :] = binit_ref[0]

    @pl.when(ci != 0)
    def _():
        up_ref[0:hist, :] = up_ref[CHUNK:CHUNK + hist, :]

    up_ref[hist:hist + CHUNK, :] = u_ref[...]

    cos = cos_ref[...]
    sin = sin_ref[...]

    def rope(t):
        return t * cos + pltpu.roll(t, dk // 2, axis=1) * sin

    heads = range(n_heads)
    qr = [rope(q_ref[:, h * dk:(h + 1) * dk]) for h in heads]
    kr = [rope(k_ref[:, h * dk:(h + 1) * dk]) * (dk ** -0.5) for h in heads]
    vb = [v_ref[:, h * dv:(h + 1) * dv].astype(bf16) for h in heads]
    scores = [_dot_nt(qr[h].astype(bf16), kr[h].astype(bf16)) * dmat_ref[h] for h in heads]
    cross = [_dot((qr[h] * qdec_ref[h]).astype(bf16), s_ref[h].astype(bf16)) for h in heads]
    inner = [_dot(scores[h].astype(bf16), vb[h]) for h in heads]
    kv = [_dot_tn((kr[h] * ktail_ref[h]).astype(bf16), vb[h]) for h in heads]
    for h in heads:
        s_ref[h] = s_ref[h] * gblk_ref[h] + kv[h]
        o = inner[h] + cross[h]
        oc = o - jnp.mean(o, axis=-1, keepdims=True)
        on = oc * lax.rsqrt(jnp.mean(oc * oc, axis=-1, keepdims=True) + EPS)
        o_ref[:, h * dv:(h + 1) * dv] = (on * _silu(g_ref[:, h * dv:(h + 1) * dv])).astype(o_ref.dtype)

    pos = (jnp.where(in_prompt, 0, PAST_LEN) + ci * CHUNK
           + lax.broadcasted_iota(jnp.int32, (CHUNK, 1), 0))
    for gi, win_len in enumerate(POOL_WINDOWS):
        cols = slice(gi * gc, (gi + 1) * gc)
        cur = up_ref[hist:hist + CHUNK, cols]
        win = cur
        for j in range(1, win_len):
            win = win + up_ref[hist - j:hist - j + CHUNK, cols]
        cnt = jnp.minimum(pos + 1, win_len).astype(f32)
        d = win / cnt - cur
        y = _dot(d.astype(bf16), pw_ref[gi]) * pscale_ref[:, cols]
        oc_ref[:, cols] = y.astype(oc_ref.dtype)

    @pl.when(is_last)
    def _():
        sout_ref[0] = s_ref[...]
        bout_ref[0] = up_ref[hist + CHUNK - nbuf:hist + CHUNK, :]


def _ret_tables(n_heads, dk, dv, n_pos):
    half = dk // 2
    inv_freq = ROPE_BASE ** (-jnp.arange(half, dtype=f32) / half)
    ang = jnp.arange(n_pos, dtype=jnp.int32).astype(f32)[:, None] * inv_freq[None, :]
    cos, sin = jnp.cos(ang), jnp.sin(ang)
    cos2 = jnp.concatenate([cos, cos], axis=-1)
    sin2 = jnp.concatenate([-sin, sin], axis=-1)
    lg = jnp.log1p(-jnp.exp2(-5.0 - jnp.arange(n_heads, dtype=f32)))
    idx = jnp.arange(CHUNK, dtype=f32)
    rel = idx[:, None] - idx[None, :]
    causal = rel >= 0
    dmat = jnp.where(causal, jnp.exp(jnp.where(causal, rel, 0.0) * lg[:, None, None]), 0.0)
    qdec = jnp.exp((idx + 1.0)[None, :] * lg[:, None])
    ktail = jnp.exp((CHUNK - 1.0 - idx)[None, :] * lg[:, None])
    gblk = jnp.exp(CHUNK * lg)
    bl = lambda t: jnp.broadcast_to(t[:, :, None], (n_heads, CHUNK, dk))
    return cos2, sin2, dmat, bl(qdec), bl(ktail), jnp.broadcast_to(gblk[:, None, None], (n_heads, 1, dv))


def _retpool_mixer(pb, pc, s_init, buf_init, layer, pool_w, pool_scale, sched, n_pos):
    m, width = pc.shape
    _, n_seq, n_heads, dk, dv = s_init.shape
    n_grp, gc, _ = pool_w.shape
    nbuf = buf_init.shape[2]
    assert dk == LANES and dv == 2 * dk
    assert n_grp == len(POOL_WINDOWS) and nbuf == max(POOL_WINDOWS) - 1 and gc % LANES == 0
    qw, vw = n_heads * dk, n_heads * dv
    cos2, sin2, dmat, qdec, ktail, gblk = _ret_tables(n_heads, dk, dv, n_pos)
    whole = lambda a: pl.BlockSpec(a.shape, lambda c: (0,) * a.ndim)
    return pl.pallas_call(
        functools.partial(_retpool_kernel, sched=sched, n_heads=n_heads, dk=dk, dv=dv, gc=gc),
        grid=(sched.n_chunks,),
        in_specs=[
            pl.BlockSpec((CHUNK, qw), lambda c: (c, 0)),
            pl.BlockSpec((CHUNK, qw), lambda c: (c, 1)),
            pl.BlockSpec((CHUNK, vw), lambda c: (c, 1)),
            pl.BlockSpec((CHUNK, vw), lambda c: (c, 2)),
            pl.BlockSpec((CHUNK, dk), lambda c: (sched.pos_chunk(c), 0)),
            pl.BlockSpec((CHUNK, dk), lambda c: (sched.pos_chunk(c), 0)),
            whole(dmat), whole(qdec), whole(ktail), whole(gblk),
            pl.BlockSpec((None, 1, n_heads, dk, dv), lambda c: (layer, sched.seq(c), 0, 0, 0)),
            pl.BlockSpec((CHUNK, width), lambda c: (c, 0)),
            pl.BlockSpec((None, 1, nbuf, width), lambda c: (layer, sched.seq(c), 0, 0)),
            pl.BlockSpec((n_grp, gc, gc), lambda c: (0, 0, 0)),
            pl.BlockSpec((1, width), lambda c: (0, 0)),
        ],
        out_specs=[
            pl.BlockSpec((CHUNK, vw), lambda c: (c, 0)),
            pl.BlockSpec((1, n_heads, dk, dv), lambda c: (sched.seq(c), 0, 0, 0)),
            pl.BlockSpec((CHUNK, width), lambda c: (c, 0)),
            pl.BlockSpec((1, nbuf, width), lambda c: (sched.seq(c), 0, 0)),
        ],
        out_shape=[jax.ShapeDtypeStruct((m, vw), bf16), jax.ShapeDtypeStruct(s_init.shape[1:], f32),
                   jax.ShapeDtypeStruct((m, width), bf16), jax.ShapeDtypeStruct((n_seq, nbuf, width), f32)],
        scratch_shapes=[pltpu.VMEM((n_heads, dk, dv), f32), pltpu.VMEM((CHUNK + 16, width), f32)],
        compiler_params=_params(("arbitrary",)),
        name="retpool_mixer",
    )(pb, pb, pb, pb, cos2, sin2, dmat, qdec, ktail, gblk, s_init, pc, buf_init, pool_w.astype(bf16),
      pool_scale.reshape(1, width))


def kernel(x_prompt, x_sample, state_conv, state_gdn, state_ret, state_pool, ffn1_norm, ffn1_w1, ffn1_w3, ffn1_w2,
           mix_norm, w_in, conv_w, gdn_a_log, gdn_dt_bias, gdn_norm_w, pool_w, pool_scale, w_out, ffn2_norm,
           ffn2_w1, ffn2_w3, ffn2_w2, final_norm):
    bp, lp, d = x_prompt.shape
    bs, ls, _ = x_sample.shape
    depth = ffn1_w1.shape[0]
    sched = _Sched(bp, lp, bs, ls)
    mp, ms = bp * lp, bs * ls
    n_pos = max(lp, PAST_LEN + ls)

    _, _, ha, dka, dva = state_gdn.shape
    _, _, hb, dkb, dvb = state_ret.shape
    wa, qkb, wb = ha * dva, hb * dkb, hb * dvb
    pool_width = state_pool.shape[-1]
    in_sizes = (2 * ha * dka + wa, wa, ha, ha, qkb, qkb, wb, wb, pool_width)
    assert sum(in_sizes) == w_in.shape[-1]
    offs = [0]
    for s in in_sizes:
        offs.append(offs[-1] + s)
    cols = lambda t, i, j: t[..., offs[i]:offs[j]]

    f_hidden = ffn1_w1.shape[-1]
    f_pad = f_hidden + (-f_hidden % FFN_TILE)
    ffns = []
    for l in range(depth):
        ffns += [(ffn1_w1, ffn1_w3, ffn1_w2, l), (ffn2_w1, ffn2_w3, ffn2_w2, l)]
    cast_up = lambda t: jnp.pad(t, ((0, 0), (0, f_pad - f_hidden))).astype(bf16)
    up_w = [cast_up(ffn1_w1[0]), cast_up(ffn1_w3[0])]

    def ffn_up_with_casts(k, hq, ss):
        w2s, layer = ffns[k][2], ffns[k][3]
        jobs = [_CastJob(w2s, layer, f_pad, d)]
        if k + 1 < len(ffns):
            n1, n3, _, nl = ffns[k + 1]
            jobs += [_CastJob(n1, nl, d, f_pad), _CastJob(n3, nl, d, f_pad)]
        hmid, casts = _ffn_up(hq, ss, up_w[0], up_w[1], 0, 0.5, tuple(jobs))
        up_w[:] = casts[1:3]
        return hmid, casts[0]

    w_a = cols(w_in, 0, 2).astype(bf16)
    w_ab = jnp.pad(cols(w_in, 2, 4), ((0, 0), (0, 0), (0, LANES - 2 * ha))).astype(bf16)
    w_b = cols(w_in, 4, 8).astype(bf16)
    w_c = cols(w_in, 8, 9).astype(bf16)
    w_o = w_out.astype(bf16)

    def stack_state(st):
        return jnp.concatenate([jnp.zeros((st.shape[0], bp) + st.shape[2:], st.dtype), st], axis=1)

    conv0, gdn0, ret0, pool0 = map(stack_state, (state_conv, state_gdn, state_ret, state_pool))

    ffn_rows = (512, 256, 128, 64)
    out_rows = (1024, 512, 256, 128, 64)

    x, hq, ss = _norm_prep(x_prompt.reshape(mp, d), x_sample.reshape(ms, d), ffn1_norm[0])
    outs = {k: [] for k in ("conv_p", "gdn_p", "ret_p", "pool_p", "conv_s", "gdn_s", "ret_s", "pool_s")}
    for l in range(depth):
        hmid, w2 = ffn_up_with_casts(2 * l, hq, ss)
        x, hq, ss = _proj_residual([hmid], w2, 0, x, mix_norm[l], ffn_rows)
        pa = _in_proj(hq, ss, w_a, l)
        pab = _in_proj(hq, ss, w_ab, l)
        pb = _in_proj(hq, ss, w_b, l)
        pc = _in_proj(hq, ss, w_c, l)
        oa, sa, ca = _gdn_mixer(pa, pab, conv0, gdn0, l, conv_w[l], gdn_a_log[l], gdn_dt_bias[l], gdn_norm_w[l],
                                sched)
        ob, sb, oc, pbuf = _retpool_mixer(pb, pc, ret0, pool0, l, pool_w[l], pool_scale[l], sched, n_pos)
        x, hq, ss = _proj_residual([oa, ob, oc], w_o, l, x, ffn2_norm[l], out_rows)
        hmid, w2 = ffn_up_with_casts(2 * l + 1, hq, ss)
        if l + 1 < depth:
            x, hq, ss = _proj_residual([hmid], w2, 0, x, ffn1_norm[l + 1], ffn_rows)
        else:
            x = _proj_residual([hmid], w2, 0, x, None, ffn_rows)
        for k, v in (("conv", ca), ("pool", pbuf), ("gdn", sa), ("ret", sb)):
            outs[k + "_p"].append(v[:bp])
            outs[k + "_s"].append(v[bp:])

    y_prompt = _final_norm(x, final_norm, 0, mp).reshape(bp, lp, d)
    y_sample = _final_norm(x, final_norm, mp, ms).reshape(bs, ls, d)
    st = {k: jnp.stack(v) for k, v in outs.items()}
    return (y_prompt, y_sample, st["conv_p"], st["gdn_p"], st["ret_p"], st["pool_p"], st["conv_s"], st["gdn_s"],
            st["ret_s"], st["pool_s"])
```

```python
import functools
import math
from typing import NamedTuple

import jax
import jax.numpy as jnp
from jax import lax
from jax.experimental import pallas as pl
from jax.experimental.pallas import tpu as pltpu

CHUNK = 64
EPS = 1e-6
ROPE_BASE = 10000.0
POOL_WINDOWS = (2, 4, 8, 16)
PAST_LEN = 4096
LANES = 128
GDN_HEAD_GROUP = 12
FFN_TILE = 512
VMEM_LIMIT_BYTES = 56 * 1024 * 1024

f32 = jnp.float32
bf16 = jnp.bfloat16
HI = lax.Precision.HIGHEST


def _pick(n, candidates):
    for c in candidates:
        if n % c == 0:
            return c
    raise ValueError(f"no block size in {candidates} divides {n}")


def _params(sem):
    return pltpu.CompilerParams(dimension_semantics=sem, vmem_limit_bytes=VMEM_LIMIT_BYTES)


def _silu(x):
    return x * jax.nn.sigmoid(x)


def _dot(a, b):
    return jnp.dot(a, b, preferred_element_type=f32)


def _dot_hi(a, b):
    return jnp.dot(a, b, preferred_element_type=f32, precision=HI)


def _dot_nt(a, b):
    return lax.dot_general(a, b, (((1,), (1,)), ((), ())), preferred_element_type=f32)


def _dot_tn(a, b):
    return lax.dot_general(a, b, (((0,), (0,)), ((), ())), preferred_element_type=f32)


def _row_rsqrt(ss_ref, d_model):
    return lax.rsqrt(ss_ref[:, 0:1] / d_model + EPS)


def _norm_prep_kernel(xa_ref, xb_ref, g_ref, x_ref, hq_ref, ss_ref, *, na):
    def emit(src_ref):
        x = src_ref[...]
        x_ref[...] = x
        hq_ref[...] = (x * g_ref[...]).astype(hq_ref.dtype)
        ss_ref[...] = jnp.broadcast_to(jnp.sum(x * x, axis=-1, keepdims=True), ss_ref.shape)

    i = pl.program_id(0)
    pl.when(i < na)(lambda: emit(xa_ref))
    pl.when(i >= na)(lambda: emit(xb_ref))


def _norm_prep(xa, xb, g):
    (ma, d), mb = xa.shape, xb.shape[0]
    bm = _pick(math.gcd(ma, mb), (256, 128, 64))
    na, m = ma // bm, ma + mb
    row = pl.BlockSpec((bm, d), lambda i: (i, 0))
    return pl.pallas_call(
        functools.partial(_norm_prep_kernel, na=na),
        grid=(m // bm,),
        in_specs=[
            pl.BlockSpec((bm, d), lambda i: (jnp.minimum(i, na - 1), 0)),
            pl.BlockSpec((bm, d), lambda i: (jnp.maximum(i - na, 0), 0)),
            pl.BlockSpec((1, d), lambda i: (0, 0)),
        ],
        out_specs=[row, row, pl.BlockSpec((bm, LANES), lambda i: (i, 0))],
        out_shape=[jax.ShapeDtypeStruct((m, d), f32), jax.ShapeDtypeStruct((m, d), bf16),
                   jax.ShapeDtypeStruct((m, LANES), f32)],
        compiler_params=_params(("arbitrary",)),
        name="norm_prep",
    )(xa, xb, g.reshape(1, d))


def _final_norm_kernel(x_ref, g_ref, o_ref):
    x = x_ref[...]
    o_ref[...] = x * lax.rsqrt(jnp.mean(x * x, axis=-1, keepdims=True) + EPS) * g_ref[...]


def _final_norm(x, g, row0, n_rows):
    d = x.shape[1]
    bm = _pick(n_rows, (512, 256, 128, 64))
    assert row0 % bm == 0
    b0 = row0 // bm
    return pl.pallas_call(
        _final_norm_kernel,
        grid=(n_rows // bm,),
        in_specs=[pl.BlockSpec((bm, d), lambda i: (i + b0, 0)), pl.BlockSpec((1, d), lambda i: (0, 0))],
        out_specs=pl.BlockSpec((bm, d), lambda i: (i, 0)),
        out_shape=jax.ShapeDtypeStruct((n_rows, d), f32),
        compiler_params=_params(("parallel",)),
        name="final_norm",
    )(x, g.reshape(1, d))


class _CastJob(NamedTuple):
    src: jax.Array
    layer: int
    rows_out: int
    windows: tuple


def _job_tile_rows(job, n_steps):
    rows = job.src.shape[1]
    for tr in (16, 32, 64, 128, 256, 512, 1024):
        if rows % tr == 0 and job.rows_out % tr == 0 and job.rows_out // tr <= n_steps:
            return tr
    raise ValueError("cast job does not fit the host grid")


def _up_kernel(*refs, d_model, scale, job_dims, nj):
    n_jobs = len(job_dims)
    h_ref, ss_ref, w1_ref, w3_ref = refs[:4]
    src_refs, o_ref, dst_refs = refs[4:4 + n_jobs], refs[4 + n_jobs], list(refs[5 + n_jobs:])
    h = h_ref[...]
    r = _row_rsqrt(ss_ref, d_model)
    g = _dot(h, w1_ref[...]) * r
    u = _dot(h, w3_ref[...]) * r
    o_ref[...] = (scale * _silu(g) * u).astype(o_ref.dtype)

    t = pl.program_id(0) * nj + pl.program_id(1)
    for (tr, rows, rows_out, windows), src_ref in zip(job_dims, src_refs):
        for col0, cols, cols_out in windows:
            dst_ref = dst_refs.pop(0)
            val = src_ref[:, col0:col0 + cols].astype(bf16)
            if rows_out > rows:
                val = jnp.where(t < rows // tr, val, jnp.zeros_like(val))
            dst_ref[:, 0:cols] = val
            if cols_out > cols:
                dst_ref[:, cols:cols_out] = jnp.zeros((tr, cols_out - cols), bf16)


def _wspec(w, layer, row0, rows, bn):
    assert row0 % rows == 0
    if w.ndim == 2:
        return pl.BlockSpec((rows, bn), lambda i, j: (row0 // rows, j))
    return pl.BlockSpec((None, rows, bn), lambda i, j: (layer, row0 // rows, j))


def _ffn_up(hq, ss, w1, w3, layer, scale, jobs=()):
    m, d = hq.shape
    f = w1.shape[-1]
    bm = _pick(m, (1024, 512, 256, 128, 64))
    bn = _pick(f, (FFN_TILE, 256, 128))
    ni, nj = m // bm, f // bn
    tile_rows = tuple(_job_tile_rows(job, ni * nj) for job in jobs)
    step = lambda i, j: i * nj + j
    in_specs = [
        pl.BlockSpec((bm, d), lambda i, j: (i, 0)),
        pl.BlockSpec((bm, LANES), lambda i, j: (i, 0)),
        _wspec(w1, layer, 0, d, bn),
        _wspec(w3, layer, 0, d, bn),
    ]
    out_specs = [pl.BlockSpec((bm, bn), lambda i, j: (i, j))]
    out_shape = [jax.ShapeDtypeStruct((m, f), bf16)]
    for job, tr in zip(jobs, tile_rows):
        n_src, n_all = job.src.shape[1] // tr, job.rows_out // tr
        in_specs.append(pl.BlockSpec(
            (None, tr, job.src.shape[2]),
            lambda i, j, n_src=n_src, layer=job.layer: (layer, jnp.minimum(step(i, j), n_src - 1), 0)))
        for _, _, cols_out in job.windows:
            out_specs.append(pl.BlockSpec(
                (tr, cols_out), lambda i, j, n_all=n_all: (jnp.minimum(step(i, j), n_all - 1), 0)))
            out_shape.append(jax.ShapeDtypeStruct((job.rows_out, cols_out), bf16))
    job_dims = tuple((tr, job.src.shape[1], job.rows_out, job.windows) for job, tr in zip(jobs, tile_rows))
    res = pl.pallas_call(
        functools.partial(_up_kernel, d_model=d, scale=scale, job_dims=job_dims, nj=nj),
        grid=(ni, nj),
        in_specs=in_specs,
        out_specs=out_specs,
        out_shape=out_shape,
        compiler_params=_params(("arbitrary", "arbitrary")),
        name="ffn_up",
    )(hq, ss, w1, w3, *[job.src for job in jobs])
    copies, k = [], 1
    for job in jobs:
        copies.append(list(res[k:k + len(job.windows)]))
        k += len(job.windows)
    return res[0], copies


def _in_proj_kernel(h_ref, ss_ref, w_ref, o_ref, *, d_model):
    o_ref[...] = _dot(h_ref[...], w_ref[...]) * _row_rsqrt(ss_ref, d_model)


def _in_proj(hq, ss, w, layer):
    m, d = hq.shape
    n = w.shape[-1]
    bm = _pick(m, (1024, 512, 256, 128, 64))
    bn = _pick(n, (1024, 768, 512, 384, 256, 128))
    return pl.pallas_call(
        functools.partial(_in_proj_kernel, d_model=d),
        grid=(m // bm, n // bn),
        in_specs=[
            pl.BlockSpec((bm, d), lambda i, j: (i, 0)),
            pl.BlockSpec((bm, LANES), lambda i, j: (i, 0)),
            _wspec(w, layer, 0, d, bn),
        ],
        out_specs=pl.BlockSpec((bm, bn), lambda i, j: (i, j)),
        out_shape=jax.ShapeDtypeStruct((m, n), f32),
        compiler_params=_params(("parallel", "arbitrary")),
        name="in_proj",
    )(hq, ss, w)


def _proj_residual_kernel(*refs, n_in, with_norm):
    a_refs, w_refs = refs[:n_in], refs[n_in:2 * n_in]
    x_ref = refs[2 * n_in]
    acc = _dot(a_refs[0][...], w_refs[0][...])
    for a_ref, w_ref in zip(a_refs[1:], w_refs[1:]):
        acc = acc + _dot(a_ref[...], w_ref[...])
    xn = x_ref[...] + acc
    if not with_norm:
        refs[2 * n_in + 1][...] = xn
        return
    g_ref, xo_ref, hq_ref, ss_ref = refs[2 * n_in + 1:]
    xo_ref[...] = xn
    hq_ref[...] = (xn * g_ref[...]).astype(hq_ref.dtype)
    part = jnp.broadcast_to(jnp.sum(xn * xn, axis=-1, keepdims=True), ss_ref.shape)
    j = pl.program_id(1)

    @pl.when(j == 0)
    def _():
        ss_ref[...] = part

    @pl.when(j != 0)
    def _():
        ss_ref[...] += part


def _proj_residual(a_list, w, layer, x, g, bm_cands):
    m, d = x.shape
    bm = _pick(m, bm_cands)
    bn = _pick(d, (512, 256, 128))
    assert w.shape[-1] == d
    with_norm = g is not None
    tile = pl.BlockSpec((bm, bn), lambda i, j: (i, j))
    in_specs = [pl.BlockSpec((bm, a.shape[1]), lambda i, j: (i, 0)) for a in a_list]
    row0 = 0
    for a in a_list:
        in_specs.append(_wspec(w, layer, row0, a.shape[1], bn))
        row0 += a.shape[1]
    assert row0 == w.shape[-2]
    in_specs.append(tile)
    args = list(a_list) + [w] * len(a_list) + [x]
    if with_norm:
        in_specs.append(pl.BlockSpec((1, bn), lambda i, j: (0, j)))
        args.append(g.reshape(1, d))
        out_specs = [tile, tile, pl.BlockSpec((bm, LANES), lambda i, j: (i, 0))]
        out_shape = [jax.ShapeDtypeStruct((m, d), f32), jax.ShapeDtypeStruct((m, d), bf16),
                     jax.ShapeDtypeStruct((m, LANES), f32)]
    else:
        out_specs, out_shape = tile, jax.ShapeDtypeStruct((m, d), f32)
    return pl.pallas_call(
        functools.partial(_proj_residual_kernel, n_in=len(a_list), with_norm=with_norm),
        grid=(m // bm, d // bn),
        in_specs=in_specs,
        out_specs=out_specs,
        out_shape=out_shape,
        compiler_params=_params(("parallel", "arbitrary")),
        name="proj_residual",
    )(*args)


class _Sched:
    def __init__(self, bp, lp, bs, ls):
        assert lp % CHUNK == 0 and ls % CHUNK == 0 and PAST_LEN % CHUNK == 0
        self.bp, self.np_, self.bs, self.ns = bp, lp // CHUNK, bs, ls // CHUNK
        self.n_prompt = self.bp * self.np_
        self.n_chunks = self.n_prompt + self.bs * self.ns
        self.n_seq = bp + bs

    def split(self, c):
        in_prompt = c < self.n_prompt
        cs = jnp.maximum(c - self.n_prompt, 0)
        seq = jnp.where(in_prompt, c // self.np_, self.bp + cs // self.ns)
        ci = jnp.where(in_prompt, c % self.np_, cs % self.ns)
        last = jnp.where(in_prompt, self.np_ - 1, self.ns - 1)
        return in_prompt, seq, ci, ci == last

    def seq(self, c):
        return self.split(c)[1]

    def pos_chunk(self, c):
        in_prompt, _, ci, _ = self.split(c)
        return jnp.where(in_prompt, 0, PAST_LEN // CHUNK) + ci


def _strict_lower_inverse_minus_eye(lows, row, col):
    pair = (row // 2) == (col // 2)
    es = [-jnp.where(pair, low, 0.0) for low in lows]
    b = 2
    while b < CHUNK:
        mask = ((row // (2 * b)) == (col // (2 * b))) & ((row // b) != (col // b))
        cs = [jnp.where(mask, low, 0.0) for low in lows]
        xs = [c + _dot(e.astype(bf16), c.astype(bf16)) for e, c in zip(es, cs)]
        ys = [x + _dot(x.astype(bf16), e.astype(bf16)) for x, e in zip(xs, es)]
        es = [e - y for e, y in zip(es, ys)]
        b *= 2
    return es


def _gdn_kernel(qkv_ref, z_ref, ab_ref, cinit_ref, sinit_ref, cw_ref, alog_ref, dtb_ref, nw_ref,
                o_ref, sout_ref, cout_ref, xp_ref, s_ref, gt_ref, *, sched, n_heads, dk, dv, phase):
    c = pl.program_id(0)
    _, _, ci, is_last = sched.split(c)
    w = n_heads * dk
    hist = 8
    kw = cw_ref.shape[0]

    if phase == "init":
        @pl.when(ci == 0)
        def _():
            xp_ref[0:hist, :] = jnp.zeros((hist, xp_ref.shape[1]), f32)
            xp_ref[hist - (kw - 1):hist, :] = cinit_ref[0]
            s_ref[...] = sinit_ref[0]

        @pl.when(ci != 0)
        def _():
            xp_ref[0:hist, :] = xp_ref[CHUNK:CHUNK + hist, :]
        return

    if phase == "finish":
        @pl.when(is_last)
        def _():
            sout_ref[0] = s_ref[...]
            cout_ref[0] = xp_ref[hist + CHUNK - (kw - 1):hist + CHUNK, :]
        return

    xp_ref[hist:hist + CHUNK, :] = qkv_ref[...]

    ab = ab_ref[...]
    sp = jnp.maximum(ab + dtb_ref[...], 0.0) + jnp.log1p(jnp.exp(-jnp.abs(ab + dtb_ref[...])))
    g_all = -jnp.exp(alog_ref[...]) * sp
    beta_all = jax.nn.sigmoid(ab)

    r2 = lax.broadcasted_iota(jnp.int32, (2 * CHUNK, CHUNK), 0)
    c2 = lax.broadcasted_iota(jnp.int32, (2 * CHUNK, CHUNK), 1)
    gp = _dot_hi((c2 <= r2).astype(f32), g_all)
    gt_ref[...] = gp.T
    g_cum = gp[0:CHUNK, :]
    eg_all = jnp.exp(g_cum)
    g_last = g_cum[CHUNK - 1:CHUNK, :]
    tail_all = jnp.exp(g_last - g_cum)
    gblk_all = jnp.exp(g_last)

    row = lax.broadcasted_iota(jnp.int32, (CHUNK, CHUNK), 0)
    col = lax.broadcasted_iota(jnp.int32, (CHUNK, CHUNK), 1)
    causal = row >= col
    strict = row > col

    def conv(off):
        xt = xp_ref[:, off:off + dk]
        acc = xt * cw_ref[0:1, off:off + dk]
        for i in range(1, kw):
            acc = pltpu.roll(acc, 1, axis=0) + xt * cw_ref[i:i + 1, off:off + dk]
        return _silu(acc[hist:hist + CHUNK, :])

    def l2n(t):
        return t * lax.rsqrt(jnp.sum(t * t, axis=-1, keepdims=True) + EPS)

    for g0 in range(0, n_heads, GDN_HEAD_GROUP):
        heads = range(g0, min(g0 + GDN_HEAD_GROUP, n_heads))
        kn = {h: l2n(conv(w + h * dk)) for h in heads}
        kb = {h: kn[h].astype(bf16) for h in heads}
        kk = {h: _dot_nt(kb[h], kb[h]) for h in heads}
        qn = {h: l2n(conv(h * dk)) * (dk ** -0.5) for h in heads}
        qk = {h: _dot_nt(qn[h].astype(bf16), kb[h]) for h in heads}
        bc = {h: beta_all[:, n_heads + h:n_heads + h + 1] for h in heads}
        egc = {h: eg_all[:, h:h + 1] for h in heads}
        decay = {h: jnp.where(causal, jnp.exp(jnp.where(
            causal, g_cum[:, h:h + 1] - gt_ref[h:h + 1, 0:CHUNK], 0.0)), 0.0) for h in heads}
        lows = [jnp.where(strict, kk[h] * decay[h] * bc[h], 0.0) for h in heads]
        es = dict(zip(heads, _strict_lower_inverse_minus_eye(lows, row, col)))
        rhs = {h: jnp.concatenate([conv(2 * w + h * dv) * bc[h], kn[h] * (bc[h] * egc[h])], axis=-1)
               for h in heads}
        sol = {h: rhs[h] + _dot(es[h].astype(bf16), rhs[h].astype(bf16)) for h in heads}

        sb = {h: s_ref[h].astype(bf16) for h in heads}
        ws = {h: _dot(sol[h][:, dv:dv + dk].astype(bf16), sb[h]) for h in heads}
        ub = {h: (sol[h][:, 0:dv] - ws[h]).astype(bf16) for h in heads}
        qs = {h: _dot((qn[h] * egc[h]).astype(bf16), sb[h]) for h in heads}
        qu = {h: _dot((qk[h] * decay[h]).astype(bf16), ub[h]) for h in heads}
        ku = {h: _dot_tn((kn[h] * tail_all[:, h:h + 1]).astype(bf16), ub[h]) for h in heads}
        for h in heads:
            s_ref[h] = s_ref[h] * gblk_all[:, h:h + 1] + ku[h]
            o = qs[h] + qu[h]
            o = o * lax.rsqrt(jnp.mean(o * o, axis=-1, keepdims=True) + EPS) * nw_ref[...]
            o = o * _silu(z_ref[:, h * dv:(h + 1) * dv])
            o_ref[:, h * dv:(h + 1) * dv] = o.astype(o_ref.dtype)


N_GDN_IN, N_RETPOOL_IN = 9, 15


def _mixer_kernel(*refs, sched, gdn_cfg, ret_cfg, gc):
    gdn_in = refs[:N_GDN_IN]
    rp_in = refs[N_GDN_IN:N_GDN_IN + N_RETPOOL_IN]
    mix_ref, sa_ref, cout_ref, sb_ref, bout_ref, xp_ref, s_a, gt_ref, s_b, up_ref = refs[N_GDN_IN + N_RETPOOL_IN:]
    wa = gdn_cfg["n_heads"] * gdn_cfg["dv"]
    wb = ret_cfg["n_heads"] * ret_cfg["dv"]
    wc = mix_ref.shape[1] - wa - wb
    for phase in ("init", "body", "finish"):
        _gdn_kernel(*gdn_in, mix_ref.at[:, pl.ds(0, wa)], sa_ref, cout_ref, xp_ref, s_a, gt_ref, sched=sched,
                    phase=phase, **gdn_cfg)
        _retpool_kernel(*rp_in, mix_ref.at[:, pl.ds(wa, wb)], sb_ref, mix_ref.at[:, pl.ds(wa + wb, wc)], bout_ref,
                        s_b, up_ref, sched=sched, gc=gc, phase=phase, **ret_cfg)


def _mixers(pa, pab, pb, pc, conv_init, gdn_init, ret_init, buf_init, layer, conv_w, a_log, dt_bias, norm_w,
            pool_w, pool_scale, sched, n_pos):
    m, width = pc.shape
    _, n_seq, ha, dka, dva = gdn_init.shape
    _, _, hb, dkb, dvb = ret_init.shape
    n_grp, gc, _ = pool_w.shape
    nbuf = buf_init.shape[2]
    kw = conv_w.shape[0]
    assert dka == dva == LANES and 2 * ha <= LANES
    assert dkb == LANES and dvb == 2 * dkb
    assert n_grp == len(POOL_WINDOWS) and nbuf == max(POOL_WINDOWS) - 1 and gc % LANES == 0
    wa, qw, vw = ha * dka, hb * dkb, hb * dvb
    pad = lambda t: jnp.pad(t.reshape(1, -1), ((0, 0), (0, LANES - t.size)))
    cos2, sin2, dmat, qdec, ktail, gblk = _ret_tables(hb, dkb, dvb, n_pos)
    whole = lambda a: pl.BlockSpec(a.shape, lambda c: (0,) * a.ndim)
    seq = sched.seq
    gdn_specs = [
        pl.BlockSpec((CHUNK, 3 * wa), lambda c: (c, 0)),
        pl.BlockSpec((CHUNK, wa), lambda c: (c, 3)),
        pl.BlockSpec((CHUNK, LANES), lambda c: (c, 0)),
        pl.BlockSpec((None, 1, kw - 1, 3 * wa), lambda c: (layer, seq(c), 0, 0)),
        pl.BlockSpec((None, 1, ha, dka, dva), lambda c: (layer, seq(c), 0, 0, 0)),
        pl.BlockSpec((kw, 3 * wa), lambda c: (0, 0)),
        pl.BlockSpec((1, LANES), lambda c: (0, 0)),
        pl.BlockSpec((1, LANES), lambda c: (0, 0)),
        pl.BlockSpec((1, dva), lambda c: (0, 0)),
    ]
    gdn_args = [pa, pa, pab, conv_init, gdn_init, conv_w, pad(a_log), pad(dt_bias), norm_w.reshape(1, dva)]
    rp_specs = [
        pl.BlockSpec((CHUNK, qw), lambda c: (c, 0)),
        pl.BlockSpec((CHUNK, qw), lambda c: (c, 1)),
        pl.BlockSpec((CHUNK, vw), lambda c: (c, 1)),
        pl.BlockSpec((CHUNK, vw), lambda c: (c, 2)),
        pl.BlockSpec((CHUNK, dkb), lambda c: (sched.pos_chunk(c), 0)),
        pl.BlockSpec((CHUNK, dkb), lambda c: (sched.pos_chunk(c), 0)),
        whole(dmat), whole(qdec), whole(ktail), whole(gblk),
        pl.BlockSpec((None, 1, hb, dkb, dvb), lambda c: (layer, seq(c), 0, 0, 0)),
        pl.BlockSpec((CHUNK, width), lambda c: (c, 0)),
        pl.BlockSpec((None, 1, nbuf, width), lambda c: (layer, seq(c), 0, 0)),
        pl.BlockSpec((n_grp, gc, gc), lambda c: (0, 0, 0)),
        pl.BlockSpec((1, width), lambda c: (0, 0)),
    ]
    rp_args = [pb, pb, pb, pb, cos2, sin2, dmat, qdec, ktail, gblk, ret_init, pc, buf_init, pool_w.astype(bf16),
               pool_scale.reshape(1, width)]
    assert len(gdn_specs) == N_GDN_IN and len(rp_specs) == N_RETPOOL_IN
    d_mix = wa + vw + width
    return pl.pallas_call(
        functools.partial(_mixer_kernel, sched=sched, gdn_cfg=dict(n_heads=ha, dk=dka, dv=dva),
                          ret_cfg=dict(n_heads=hb, dk=dkb, dv=dvb), gc=gc),
        grid=(sched.n_chunks,),
        in_specs=gdn_specs + rp_specs,
        out_specs=[
            pl.BlockSpec((CHUNK, d_mix), lambda c: (c, 0)),
            pl.BlockSpec((1, ha, dka, dva), lambda c: (seq(c), 0, 0, 0)),
            pl.BlockSpec((1, kw - 1, 3 * wa), lambda c: (seq(c), 0, 0)),
            pl.BlockSpec((1, hb, dkb, dvb), lambda c: (seq(c), 0, 0, 0)),
            pl.BlockSpec((1, nbuf, width), lambda c: (seq(c), 0, 0)),
        ],
        out_shape=[
            jax.ShapeDtypeStruct((m, d_mix), bf16),
            jax.ShapeDtypeStruct(gdn_init.shape[1:], f32),
            jax.ShapeDtypeStruct((n_seq, kw - 1, 3 * wa), f32),
            jax.ShapeDtypeStruct(ret_init.shape[1:], f32),
            jax.ShapeDtypeStruct((n_seq, nbuf, width), f32),
        ],
        scratch_shapes=[
            pltpu.VMEM((CHUNK + 8, 3 * wa), f32),
            pltpu.VMEM((ha, dka, dva), f32),
            pltpu.VMEM((LANES, 2 * CHUNK), f32),
            pltpu.VMEM((hb, dkb, dvb), f32),
            pltpu.VMEM((CHUNK + 16, width), f32),
        ],
        compiler_params=_params(("arbitrary",)),
        name="mixers",
    )(*gdn_args, *rp_args)


def _retpool_kernel(q_ref, k_ref, v_ref, g_ref, cos_ref, sin_ref, dmat_ref, qdec_ref, ktail_ref, gblk_ref, sinit_ref,
                    u_ref, binit_ref, pw_ref, pscale_ref, o_ref, sout_ref, oc_ref, bout_ref, s_ref, up_ref,
                    *, sched, n_heads, dk, dv, gc, phase):
    c = pl.program_id(0)
    in_prompt, _, ci, is_last = sched.split(c)
    hist = 16
    nbuf = binit_ref.shape[1]

    if phase == "init":
        @pl.when(ci == 0)
        def _():
            s_ref[...] = sinit_ref[0]
            up_ref[0:hist, :] = jnp.zeros((hist, up_ref.shape[1]), f32)
            up_ref[hist - nbuf:hist, :] = binit_ref[0]

        @pl.when(ci != 0)
        def _():
            up_ref[0:hist, :] = up_ref[CHUNK:CHUNK + hist, :]
        return

    if phase == "finish":
        @pl.when(is_last)
        def _():
            sout_ref[0] = s_ref[...]
            bout_ref[0] = up_ref[hist + CHUNK - nbuf:hist + CHUNK, :]
        return

    up_ref[hist:hist + CHUNK, :] = u_ref[...]

    cos = cos_ref[...]
    sin = sin_ref[...]

    def rope(t):
        return t * cos + pltpu.roll(t, dk // 2, axis=1) * sin

    heads = range(n_heads)
    qr = [rope(q_ref[:, h * dk:(h + 1) * dk]) for h in heads]
    kr = [rope(k_ref[:, h * dk:(h + 1) * dk]) * (dk ** -0.5) for h in heads]
    vb = [v_ref[:, h * dv:(h + 1) * dv].astype(bf16) for h in heads]
    scores = [_dot_nt(qr[h].astype(bf16), kr[h].astype(bf16)) * dmat_ref[h] for h in heads]
    cross = [_dot((qr[h] * qdec_ref[h]).astype(bf16), s_ref[h].astype(bf16)) for h in heads]
    inner = [_dot(scores[h].astype(bf16), vb[h]) for h in heads]
    kv = [_dot_tn((kr[h] * ktail_ref[h]).astype(bf16), vb[h]) for h in heads]
    for h in heads:
        s_ref[h] = s_ref[h] * gblk_ref[h] + kv[h]
        o = inner[h] + cross[h]
        oc = o - jnp.mean(o, axis=-1, keepdims=True)
        on = oc * lax.rsqrt(jnp.mean(oc * oc, axis=-1, keepdims=True) + EPS)
        o_ref[:, h * dv:(h + 1) * dv] = (on * _silu(g_ref[:, h * dv:(h + 1) * dv])).astype(o_ref.dtype)

    pos = (jnp.where(in_prompt, 0, PAST_LEN) + ci * CHUNK
           + lax.broadcasted_iota(jnp.int32, (CHUNK, 1), 0))
    for gi, win_len in enumerate(POOL_WINDOWS):
        cols = slice(gi * gc, (gi + 1) * gc)
        ext = up_ref[:, cols]
        acc, span = ext, 1
        while span < win_len:
            acc = acc + pltpu.roll(acc, span, axis=0)
            span *= 2
        assert span == win_len and win_len - 1 <= hist
        cur = ext[hist:hist + CHUNK, :]
        cnt = jnp.minimum(pos + 1, win_len).astype(f32)
        d = acc[hist:hist + CHUNK, :] / cnt - cur
        y = _dot(d.astype(bf16), pw_ref[gi]) * pscale_ref[:, cols]
        oc_ref[:, cols] = y.astype(oc_ref.dtype)


def _ret_tables(n_heads, dk, dv, n_pos):
    half = dk // 2
    inv_freq = ROPE_BASE ** (-jnp.arange(half, dtype=f32) / half)
    ang = jnp.arange(n_pos, dtype=jnp.int32).astype(f32)[:, None] * inv_freq[None, :]
    cos, sin = jnp.cos(ang), jnp.sin(ang)
    cos2 = jnp.concatenate([cos, cos], axis=-1)
    sin2 = jnp.concatenate([-sin, sin], axis=-1)
    lg = jnp.log1p(-jnp.exp2(-5.0 - jnp.arange(n_heads, dtype=f32)))
    idx = jnp.arange(CHUNK, dtype=f32)
    rel = idx[:, None] - idx[None, :]
    causal = rel >= 0
    dmat = jnp.where(causal, jnp.exp(jnp.where(causal, rel, 0.0) * lg[:, None, None]), 0.0)
    qdec = jnp.exp((idx + 1.0)[None, :] * lg[:, None])
    ktail = jnp.exp((CHUNK - 1.0 - idx)[None, :] * lg[:, None])
    gblk = jnp.exp(CHUNK * lg)
    bl = lambda t: jnp.broadcast_to(t[:, :, None], (n_heads, CHUNK, dk))
    return cos2, sin2, dmat, bl(qdec), bl(ktail), jnp.broadcast_to(gblk[:, None, None], (n_heads, 1, dv))


def kernel(x_prompt, x_sample, state_conv, state_gdn, state_ret, state_pool, ffn1_norm, ffn1_w1, ffn1_w3, ffn1_w2,
           mix_norm, w_in, conv_w, gdn_a_log, gdn_dt_bias, gdn_norm_w, pool_w, pool_scale, w_out, ffn2_norm,
           ffn2_w1, ffn2_w3, ffn2_w2, final_norm):
    bp, lp, d = x_prompt.shape
    bs, ls, _ = x_sample.shape
    depth = ffn1_w1.shape[0]
    sched = _Sched(bp, lp, bs, ls)
    mp, ms = bp * lp, bs * ls
    n_pos = max(lp, PAST_LEN + ls)

    _, _, ha, dka, dva = state_gdn.shape
    _, _, hb, dkb, dvb = state_ret.shape
    wa, qkb, wb = ha * dva, hb * dkb, hb * dvb
    pool_width = state_pool.shape[-1]
    in_sizes = (2 * ha * dka + wa, wa, ha, ha, qkb, qkb, wb, wb, pool_width)
    assert sum(in_sizes) == w_in.shape[-1]
    offs = [0]
    for s in in_sizes:
        offs.append(offs[-1] + s)

    f_hidden = ffn1_w1.shape[-1]
    f_pad = f_hidden + (-f_hidden % FFN_TILE)
    ffns = []
    for l in range(depth):
        ffns += [(ffn1_w1, ffn1_w3, ffn1_w2, l), (ffn2_w1, ffn2_w3, ffn2_w2, l)]
    cast_up = lambda t: jnp.pad(t, ((0, 0), (0, f_pad - f_hidden))).astype(bf16)
    up_w = [cast_up(ffn1_w1[0]), cast_up(ffn1_w3[0])]
    span = lambda i, j, out=None: (offs[i], offs[j] - offs[i], out or offs[j] - offs[i])
    in_windows = (span(0, 2), span(2, 4, LANES), span(4, 8), span(8, 9))
    mix_w = {}

    def ffn_up_with_casts(k, hq, ss):
        w2s, layer = ffns[k][2], ffns[k][3]
        jobs = [_CastJob(w2s, layer, f_pad, ((0, d, d),))]
        if k + 1 < len(ffns):
            n1, n3, _, nl = ffns[k + 1]
            jobs += [_CastJob(n1, nl, d, ((0, f_hidden, f_pad),)), _CastJob(n3, nl, d, ((0, f_hidden, f_pad),))]
        mix_layer = 0 if k == 0 else (k + 1) // 2 if k % 2 == 1 and (k + 1) // 2 < depth else None
        if mix_layer is not None:
            jobs += [_CastJob(w_in, mix_layer, d, in_windows),
                     _CastJob(w_out, mix_layer, w_out.shape[1], ((0, d, d),))]
        hmid, casts = _ffn_up(hq, ss, up_w[0], up_w[1], 0, 0.5, tuple(jobs))
        if k + 1 < len(ffns):
            up_w[:] = [casts[1][0], casts[2][0]]
        if mix_layer is not None:
            mix_w[mix_layer] = casts[-2] + casts[-1]
        return hmid, casts[0][0]

    def stack_state(st):
        return jnp.concatenate([jnp.zeros((st.shape[0], bp) + st.shape[2:], st.dtype), st], axis=1)

    conv0, gdn0, ret0, pool0 = map(stack_state, (state_conv, state_gdn, state_ret, state_pool))

    ffn_rows = (512, 256, 128, 64)
    out_rows = (1024, 512, 256, 128, 64)

    x, hq, ss = _norm_prep(x_prompt.reshape(mp, d), x_sample.reshape(ms, d), ffn1_norm[0])
    outs = {k: [] for k in ("conv_p", "gdn_p", "ret_p", "pool_p", "conv_s", "gdn_s", "ret_s", "pool_s")}
    for l in range(depth):
        hmid, w2 = ffn_up_with_casts(2 * l, hq, ss)
        x, hq, ss = _proj_residual([hmid], w2, 0, x, mix_norm[l], ffn_rows)
        w_a, w_ab, w_b, w_c, w_o = mix_w[l]
        pa, pab, pb, pc = (_in_proj(hq, ss, w, 0) for w in (w_a, w_ab, w_b, w_c))
        mix, sa, ca, sb, pbuf = _mixers(pa, pab, pb, pc, conv0, gdn0, ret0, pool0, l, conv_w[l], gdn_a_log[l],
                                        gdn_dt_bias[l], gdn_norm_w[l], pool_w[l], pool_scale[l], sched, n_pos)
        x, hq, ss = _proj_residual([mix], w_o, 0, x, ffn2_norm[l], out_rows)
        hmid, w2 = ffn_up_with_casts(2 * l + 1, hq, ss)
        if l + 1 < depth:
            x, hq, ss = _proj_residual([hmid], w2, 0, x, ffn1_norm[l + 1], ffn_rows)
        else:
            x = _proj_residual([hmid], w2, 0, x, None, ffn_rows)
        for k, v in (("conv", ca), ("pool", pbuf), ("gdn", sa), ("ret", sb)):
            outs[k + "_p"].append(v[:bp])
            outs[k + "_s"].append(v[bp:])

    y_prompt = _final_norm(x, final_norm, 0, mp).reshape(bp, lp, d)
    y_sample = _final_norm(x, final_norm, mp, ms).reshape(bs, ls, d)
    st = {k: jnp.stack(v) for k, v in outs.items()}
    return (y_prompt, y_sample, st["conv_p"], st["gdn_p"], st["ret_p"], st["pool_p"], st["conv_s"], st["gdn_s"],
            st["ret_s"], st["pool_s"])
```

```python
import functools
import math
from typing import NamedTuple

import jax
import jax.numpy as jnp
from jax import lax
from jax.experimental import pallas as pl
from jax.experimental.pallas import tpu as pltpu

CHUNK = 64
EPS = 1e-6
ROPE_BASE = 10000.0
POOL_WINDOWS = (2, 4, 8, 16)
PAST_LEN = 4096
LANES = 128
GDN_HEAD_GROUP = 12
FFN_TILE = 512
VMEM_LIMIT_BYTES = 56 * 1024 * 1024

f32 = jnp.float32
bf16 = jnp.bfloat16
HI = lax.Precision.HIGHEST


def _pick(n, candidates):
    for c in candidates:
        if n % c == 0:
            return c
    raise ValueError(f"no block size in {candidates} divides {n}")


def _params(sem):
    return pltpu.CompilerParams(dimension_semantics=sem, vmem_limit_bytes=VMEM_LIMIT_BYTES)


def _silu(x):
    return x * jax.nn.sigmoid(x)


def _dot(a, b):
    return jnp.dot(a, b, preferred_element_type=f32)


def _dot_hi(a, b):
    return jnp.dot(a, b, preferred_element_type=f32, precision=HI)


def _dot_nt(a, b):
    return lax.dot_general(a, b, (((1,), (1,)), ((), ())), preferred_element_type=f32)


def _dot_tn(a, b):
    return lax.dot_general(a, b, (((0,), (0,)), ((), ())), preferred_element_type=f32)


def _row_rsqrt(ss_ref, d_model):
    return lax.rsqrt(ss_ref[:, 0:1] / d_model + EPS)


def _norm_prep_kernel(*refs, na, job_dims):
    n_jobs = len(job_dims)
    xa_ref, xb_ref, g_ref = refs[:3]
    src_refs = refs[3:3 + n_jobs]
    x_ref, hq_ref, ss_ref = refs[3 + n_jobs:6 + n_jobs]
    dst_refs = refs[6 + n_jobs:]

    def emit(src_ref):
        x = src_ref[...]
        x_ref[...] = x
        hq_ref[...] = (x * g_ref[...]).astype(hq_ref.dtype)
        ss_ref[...] = jnp.broadcast_to(jnp.sum(x * x, axis=-1, keepdims=True), ss_ref.shape)

    i = pl.program_id(0)
    pl.when(i < na)(lambda: emit(xa_ref))
    pl.when(i >= na)(lambda: emit(xb_ref))
    _run_jobs(job_dims, src_refs, dst_refs, i)


def _norm_prep(xa, xb, g, jobs=()):
    (ma, d), mb = xa.shape, xb.shape[0]
    bm = _pick(math.gcd(ma, mb), (256, 128, 64))
    na, m = ma // bm, ma + mb
    job_in, job_out, job_shape, job_dims = _job_plan(jobs, m // bm, lambda i: i)
    row = pl.BlockSpec((bm, d), lambda i: (i, 0))
    res = pl.pallas_call(
        functools.partial(_norm_prep_kernel, na=na, job_dims=job_dims),
        grid=(m // bm,),
        in_specs=[
            pl.BlockSpec((bm, d), lambda i: (jnp.minimum(i, na - 1), 0)),
            pl.BlockSpec((bm, d), lambda i: (jnp.maximum(i - na, 0), 0)),
            pl.BlockSpec((1, d), lambda i: (0, 0)),
        ] + job_in,
        out_specs=[row, row, pl.BlockSpec((bm, LANES), lambda i: (i, 0))] + job_out,
        out_shape=[jax.ShapeDtypeStruct((m, d), f32), jax.ShapeDtypeStruct((m, d), bf16),
                   jax.ShapeDtypeStruct((m, LANES), f32)] + job_shape,
        compiler_params=_params(("arbitrary",)),
        name="norm_prep",
    )(xa, xb, g.reshape(1, d), *[job.src for job in jobs])
    return res[0], res[1], res[2], _split_copies(jobs, res[3:])


def _final_norm_kernel(x_ref, g_ref, o_ref):
    x = x_ref[...]
    o_ref[...] = x * lax.rsqrt(jnp.mean(x * x, axis=-1, keepdims=True) + EPS) * g_ref[...]


def _final_norm(x, g, row0, n_rows):
    d = x.shape[1]
    bm = _pick(n_rows, (512, 256, 128, 64))
    assert row0 % bm == 0
    b0 = row0 // bm
    return pl.pallas_call(
        _final_norm_kernel,
        grid=(n_rows // bm,),
        in_specs=[pl.BlockSpec((bm, d), lambda i: (i + b0, 0)), pl.BlockSpec((1, d), lambda i: (0, 0))],
        out_specs=pl.BlockSpec((bm, d), lambda i: (i, 0)),
        out_shape=jax.ShapeDtypeStruct((n_rows, d), f32),
        compiler_params=_params(("parallel",)),
        name="final_norm",
    )(x, g.reshape(1, d))


class _CastJob(NamedTuple):
    src: jax.Array
    layer: int
    rows_out: int
    windows: tuple


def _job_tile_rows(job, n_steps):
    rows = job.src.shape[1]
    for tr in (16, 32, 64, 128, 256, 512, 1024):
        if rows % tr == 0 and job.rows_out % tr == 0 and job.rows_out // tr <= n_steps:
            return tr
    raise ValueError("cast job does not fit the host grid")


def _job_plan(jobs, n_steps, step_of):
    in_specs, out_specs, out_shape, dims = [], [], [], []
    for job in jobs:
        tr = _job_tile_rows(job, n_steps)
        n_src, n_all = job.src.shape[1] // tr, job.rows_out // tr
        in_specs.append(pl.BlockSpec(
            (None, tr, job.src.shape[2]),
            lambda *g, n_src=n_src, layer=job.layer: (layer, jnp.minimum(step_of(*g), n_src - 1), 0)))
        for _, _, cols_out in job.windows:
            out_specs.append(pl.BlockSpec(
                (tr, cols_out), lambda *g, n_all=n_all: (jnp.minimum(step_of(*g), n_all - 1), 0)))
            out_shape.append(jax.ShapeDtypeStruct((job.rows_out, cols_out), bf16))
        dims.append((tr, job.src.shape[1], job.rows_out, job.windows))
    return in_specs, out_specs, out_shape, tuple(dims)


def _run_jobs(job_dims, src_refs, dst_refs, t):
    dst_refs = list(dst_refs)
    for (tr, rows, rows_out, windows), src_ref in zip(job_dims, src_refs):
        for col0, cols, cols_out in windows:
            dst_ref = dst_refs.pop(0)
            val = src_ref[:, col0:col0 + cols].astype(bf16)
            if rows_out > rows:
                val = jnp.where(t < rows // tr, val, jnp.zeros_like(val))
            dst_ref[:, 0:cols] = val
            if cols_out > cols:
                dst_ref[:, cols:cols_out] = jnp.zeros((tr, cols_out - cols), bf16)


def _split_copies(jobs, outs):
    copies, k = [], 0
    for job in jobs:
        copies.append(list(outs[k:k + len(job.windows)]))
        k += len(job.windows)
    return copies


def _up_kernel(*refs, d_model, scale, job_dims, nj):
    n_jobs = len(job_dims)
    h_ref, ss_ref, w1_ref, w3_ref = refs[:4]
    src_refs, o_ref, dst_refs = refs[4:4 + n_jobs], refs[4 + n_jobs], refs[5 + n_jobs:]
    h = h_ref[...]
    r = _row_rsqrt(ss_ref, d_model)
    g = _dot(h, w1_ref[...]) * r
    u = _dot(h, w3_ref[...]) * r
    o_ref[...] = (scale * _silu(g) * u).astype(o_ref.dtype)
    _run_jobs(job_dims, src_refs, dst_refs, pl.program_id(0) * nj + pl.program_id(1))


def _wspec(w, layer, row0, rows, bn):
    assert row0 % rows == 0
    if w.ndim == 2:
        return pl.BlockSpec((rows, bn), lambda i, j: (row0 // rows, j))
    return pl.BlockSpec((None, rows, bn), lambda i, j: (layer, row0 // rows, j))


def _ffn_up(hq, ss, w1, w3, layer, scale, jobs=()):
    m, d = hq.shape
    f = w1.shape[-1]
    bm = _pick(m, (1024, 512, 256, 128, 64))
    bn = _pick(f, (FFN_TILE, 256, 128))
    ni, nj = m // bm, f // bn
    job_in, job_out, job_shape, job_dims = _job_plan(jobs, ni * nj, lambda i, j: i * nj + j)
    in_specs = [
        pl.BlockSpec((bm, d), lambda i, j: (i, 0)),
        pl.BlockSpec((bm, LANES), lambda i, j: (i, 0)),
        _wspec(w1, layer, 0, d, bn),
        _wspec(w3, layer, 0, d, bn),
    ]
    res = pl.pallas_call(
        functools.partial(_up_kernel, d_model=d, scale=scale, job_dims=job_dims, nj=nj),
        grid=(ni, nj),
        in_specs=in_specs + job_in,
        out_specs=[pl.BlockSpec((bm, bn), lambda i, j: (i, j))] + job_out,
        out_shape=[jax.ShapeDtypeStruct((m, f), bf16)] + job_shape,
        compiler_params=_params(("arbitrary", "arbitrary")),
        name="ffn_up",
    )(hq, ss, w1, w3, *[job.src for job in jobs])
    return res[0], _split_copies(jobs, res[1:])


def _in_proj_kernel(h_ref, ss_ref, w_ref, o_ref, *, d_model):
    o_ref[...] = _dot(h_ref[...], w_ref[...]) * _row_rsqrt(ss_ref, d_model)


def _in_proj(hq, ss, w, layer):
    m, d = hq.shape
    n = w.shape[-1]
    bm = _pick(m, (1024, 512, 256, 128, 64))
    bn = _pick(n, (1024, 768, 512, 384, 256, 128))
    return pl.pallas_call(
        functools.partial(_in_proj_kernel, d_model=d),
        grid=(m // bm, n // bn),
        in_specs=[
            pl.BlockSpec((bm, d), lambda i, j: (i, 0)),
            pl.BlockSpec((bm, LANES), lambda i, j: (i, 0)),
            _wspec(w, layer, 0, d, bn),
        ],
        out_specs=pl.BlockSpec((bm, bn), lambda i, j: (i, j)),
        out_shape=jax.ShapeDtypeStruct((m, n), f32),
        compiler_params=_params(("parallel", "arbitrary")),
        name="in_proj",
    )(hq, ss, w)


def _proj_residual_kernel(*refs, n_in, with_norm):
    a_refs, w_refs = refs[:n_in], refs[n_in:2 * n_in]
    x_ref = refs[2 * n_in]
    acc = _dot(a_refs[0][...], w_refs[0][...])
    for a_ref, w_ref in zip(a_refs[1:], w_refs[1:]):
        acc = acc + _dot(a_ref[...], w_ref[...])
    xn = x_ref[...] + acc
    if not with_norm:
        refs[2 * n_in + 1][...] = xn
        return
    g_ref, xo_ref, hq_ref, ss_ref = refs[2 * n_in + 1:]
    xo_ref[...] = xn
    hq_ref[...] = (xn * g_ref[...]).astype(hq_ref.dtype)
    part = jnp.broadcast_to(jnp.sum(xn * xn, axis=-1, keepdims=True), ss_ref.shape)
    j = pl.program_id(1)

    @pl.when(j == 0)
    def _():
        ss_ref[...] = part

    @pl.when(j != 0)
    def _():
        ss_ref[...] += part


def _proj_residual(a_list, w, layer, x, g, bm_cands):
    m, d = x.shape
    bm = _pick(m, bm_cands)
    bn = _pick(d, (512, 256, 128))
    assert w.shape[-1] == d
    with_norm = g is not None
    tile = pl.BlockSpec((bm, bn), lambda i, j: (i, j))
    in_specs = [pl.BlockSpec((bm, a.shape[1]), lambda i, j: (i, 0)) for a in a_list]
    row0 = 0
    for a in a_list:
        in_specs.append(_wspec(w, layer, row0, a.shape[1], bn))
        row0 += a.shape[1]
    assert row0 == w.shape[-2]
    in_specs.append(tile)
    args = list(a_list) + [w] * len(a_list) + [x]
    if with_norm:
        in_specs.append(pl.BlockSpec((1, bn), lambda i, j: (0, j)))
        args.append(g.reshape(1, d))
        out_specs = [tile, tile, pl.BlockSpec((bm, LANES), lambda i, j: (i, 0))]
        out_shape = [jax.ShapeDtypeStruct((m, d), f32), jax.ShapeDtypeStruct((m, d), bf16),
                     jax.ShapeDtypeStruct((m, LANES), f32)]
    else:
        out_specs, out_shape = tile, jax.ShapeDtypeStruct((m, d), f32)
    return pl.pallas_call(
        functools.partial(_proj_residual_kernel, n_in=len(a_list), with_norm=with_norm),
        grid=(m // bm, d // bn),
        in_specs=in_specs,
        out_specs=out_specs,
        out_shape=out_shape,
        compiler_params=_params(("parallel", "arbitrary")),
        name="proj_residual",
    )(*args)


class _Sched:
    def __init__(self, bp, lp, bs, ls):
        assert lp % CHUNK == 0 and ls % CHUNK == 0 and PAST_LEN % CHUNK == 0
        self.bp, self.np_, self.bs, self.ns = bp, lp // CHUNK, bs, ls // CHUNK
        self.n_prompt = self.bp * self.np_
        self.n_chunks = self.n_prompt + self.bs * self.ns
        self.n_seq = bp + bs

    def split(self, c):
        in_prompt = c < self.n_prompt
        cs = jnp.maximum(c - self.n_prompt, 0)
        seq = jnp.where(in_prompt, c // self.np_, self.bp + cs // self.ns)
        ci = jnp.where(in_prompt, c % self.np_, cs % self.ns)
        last = jnp.where(in_prompt, self.np_ - 1, self.ns - 1)
        return in_prompt, seq, ci, ci == last

    def seq(self, c):
        return self.split(c)[1]

    def sample_seq(self, c):
        return jnp.maximum(self.seq(c) - self.bp, 0)

    def pos_chunk(self, c):
        in_prompt, _, ci, _ = self.split(c)
        return jnp.where(in_prompt, 0, PAST_LEN // CHUNK) + ci


def _strict_lower_inverse_minus_eye(lows, row, col):
    pair = (row // 2) == (col // 2)
    es = [-jnp.where(pair, low, 0.0) for low in lows]
    b = 2
    while b < CHUNK:
        mask = ((row // (2 * b)) == (col // (2 * b))) & ((row // b) != (col // b))
        cs = [jnp.where(mask, low, 0.0) for low in lows]
        xs = [c + _dot(e.astype(bf16), c.astype(bf16)) for e, c in zip(es, cs)]
        ys = [x + _dot(x.astype(bf16), e.astype(bf16)) for x, e in zip(xs, es)]
        es = [e - y for e, y in zip(es, ys)]
        b *= 2
    return es


def _gdn_kernel(qkv_ref, z_ref, ab_ref, cinit_ref, sinit_ref, cw_ref, alog_ref, dtb_ref, nw_ref,
                o_ref, sout_ref, cout_ref, hist_ref, s_ref, gt_ref, *, sched, n_heads, dk, dv, phase):
    c = pl.program_id(0)
    in_prompt, _, ci, is_last = sched.split(c)
    w = n_heads * dk
    hist = hist_ref.shape[0]
    kw = cw_ref.shape[0]

    if phase == "init":
        @pl.when((ci == 0) & in_prompt)
        def _():
            hist_ref[...] = jnp.zeros(hist_ref.shape, f32)
            s_ref[...] = jnp.zeros(s_ref.shape, f32)

        @pl.when((ci == 0) & jnp.logical_not(in_prompt))
        def _():
            hist_ref[0:hist - (kw - 1), :] = jnp.zeros((hist - (kw - 1), hist_ref.shape[1]), f32)
            hist_ref[hist - (kw - 1):hist, :] = cinit_ref[0]
            s_ref[...] = sinit_ref[0]
        return

    if phase == "finish":
        @pl.when(is_last)
        def _():
            sout_ref[0] = s_ref[...]
            cout_ref[0] = qkv_ref[CHUNK - (kw - 1):CHUNK, :]
        return

    ab = ab_ref[...]
    sp = jnp.maximum(ab + dtb_ref[...], 0.0) + jnp.log1p(jnp.exp(-jnp.abs(ab + dtb_ref[...])))
    g_all = -jnp.exp(alog_ref[...]) * sp
    beta_all = jax.nn.sigmoid(ab)

    r2 = lax.broadcasted_iota(jnp.int32, (2 * CHUNK, CHUNK), 0)
    c2 = lax.broadcasted_iota(jnp.int32, (2 * CHUNK, CHUNK), 1)
    gp = _dot_hi((c2 <= r2).astype(f32), g_all)
    gt_ref[...] = gp.T
    g_cum = gp[0:CHUNK, :]
    eg_all = jnp.exp(g_cum)
    g_last = g_cum[CHUNK - 1:CHUNK, :]
    tail_all = jnp.exp(g_last - g_cum)
    gblk_all = jnp.exp(g_last)

    row = lax.broadcasted_iota(jnp.int32, (CHUNK, CHUNK), 0)
    col = lax.broadcasted_iota(jnp.int32, (CHUNK, CHUNK), 1)
    causal = row >= col
    strict = row > col

    def conv(off):
        xt = jnp.concatenate([hist_ref[:, off:off + dk], qkv_ref[:, off:off + dk]], axis=0)
        acc = xt * cw_ref[0:1, off:off + dk]
        for i in range(1, kw):
            acc = pltpu.roll(acc, 1, axis=0) + xt * cw_ref[i:i + 1, off:off + dk]
        return _silu(acc[hist:hist + CHUNK, :])

    def l2n(t, scale=1.0):
        return t * (lax.rsqrt(jnp.sum(t * t, axis=-1, keepdims=True) + EPS) * scale)

    for g0 in range(0, n_heads, GDN_HEAD_GROUP):
        heads = range(g0, min(g0 + GDN_HEAD_GROUP, n_heads))
        lows, rhs_k, ktail_b, qdec_b, qkd_b, bc = [], {}, {}, {}, {}, {}
        for h in heads:
            bc[h] = beta_all[:, n_heads + h:n_heads + h + 1]
            egc = eg_all[:, h:h + 1]
            kn = l2n(conv(w + h * dk))
            kb = kn.astype(bf16)
            kk = _dot_nt(kb, kb)
            qn = l2n(conv(h * dk), dk ** -0.5)
            qk = _dot_nt(qn.astype(bf16), kb)
            decay = jnp.where(causal, jnp.exp(jnp.where(
                causal, g_cum[:, h:h + 1] - gt_ref[h:h + 1, 0:CHUNK], 0.0)), 0.0)
            lows.append(jnp.where(strict, kk * decay * bc[h], 0.0))
            qkd_b[h] = (qk * decay).astype(bf16)
            qdec_b[h] = (qn * egc).astype(bf16)
            ktail_b[h] = (kn * tail_all[:, h:h + 1]).astype(bf16)
            rhs_k[h] = kn * (bc[h] * egc)
        es = dict(zip(heads, _strict_lower_inverse_minus_eye(lows, row, col)))
        rhs = {h: jnp.concatenate([conv(2 * w + h * dv) * bc[h], rhs_k[h]], axis=-1) for h in heads}
        sol = {h: rhs[h] + _dot(es[h].astype(bf16), rhs[h].astype(bf16)) for h in heads}

        sb = {h: s_ref[h].astype(bf16) for h in heads}
        ws = {h: _dot(sol[h][:, dv:dv + dk].astype(bf16), sb[h]) for h in heads}
        ub = {h: (sol[h][:, 0:dv] - ws[h]).astype(bf16) for h in heads}
        qs = {h: _dot(qdec_b[h], sb[h]) for h in heads}
        qu = {h: _dot(qkd_b[h], ub[h]) for h in heads}
        ku = {h: _dot_tn(ktail_b[h], ub[h]) for h in heads}
        for h in heads:
            s_ref[h] = s_ref[h] * gblk_all[:, h:h + 1] + ku[h]
            o = qs[h] + qu[h]
            o = o * lax.rsqrt(jnp.mean(o * o, axis=-1, keepdims=True) + EPS) * nw_ref[...]
            o = o * _silu(z_ref[:, h * dv:(h + 1) * dv])
            o_ref[:, h * dv:(h + 1) * dv] = o.astype(o_ref.dtype)
    hist_ref[...] = qkv_ref[CHUNK - hist:CHUNK, :]


N_GDN_IN, N_RETPOOL_IN = 9, 15


def _mixer_kernel(*refs, sched, gdn_cfg, ret_cfg, gc):
    gdn_in = refs[:N_GDN_IN]
    rp_in = refs[N_GDN_IN:N_GDN_IN + N_RETPOOL_IN]
    mix_ref, sa_ref, cout_ref, sb_ref, bout_ref, xp_ref, s_a, gt_ref, s_b, up_ref = refs[N_GDN_IN + N_RETPOOL_IN:]
    wa = gdn_cfg["n_heads"] * gdn_cfg["dv"]
    wb = ret_cfg["n_heads"] * ret_cfg["dv"]
    wc = mix_ref.shape[1] - wa - wb
    for phase in ("init", "body", "finish"):
        _gdn_kernel(*gdn_in, mix_ref.at[:, pl.ds(0, wa)], sa_ref, cout_ref, xp_ref, s_a, gt_ref, sched=sched,
                    phase=phase, **gdn_cfg)
        _retpool_kernel(*rp_in, mix_ref.at[:, pl.ds(wa, wb)], sb_ref, mix_ref.at[:, pl.ds(wa + wb, wc)], bout_ref,
                        s_b, up_ref, sched=sched, gc=gc, phase=phase, **ret_cfg)


def _mixers(pa, pab, pb, pc, conv_init, gdn_init, ret_init, buf_init, layer, conv_w, a_log, dt_bias, norm_w,
            pool_w, pool_scale, sched, n_pos):
    m, width = pc.shape
    _, _, ha, dka, dva = gdn_init.shape
    _, _, hb, dkb, dvb = ret_init.shape
    n_seq = sched.n_seq
    assert gdn_init.shape[1] == sched.bs
    n_grp, gc, _ = pool_w.shape
    nbuf = buf_init.shape[2]
    kw = conv_w.shape[0]
    assert dka == dva == LANES and 2 * ha <= LANES
    assert dkb == LANES and dvb == 2 * dkb
    assert n_grp == len(POOL_WINDOWS) and nbuf == max(POOL_WINDOWS) - 1 and gc % LANES == 0
    wa, qw, vw = ha * dka, hb * dkb, hb * dvb
    pad = lambda t: jnp.pad(t.reshape(1, -1), ((0, 0), (0, LANES - t.size)))
    cos2, sin2, dmat, qdec, ktail, gblk = _ret_tables(hb, dkb, dvb, n_pos)
    whole = lambda a: pl.BlockSpec(a.shape, lambda c: (0,) * a.ndim)
    seq = sched.seq
    gdn_specs = [
        pl.BlockSpec((CHUNK, 3 * wa), lambda c: (c, 0)),
        pl.BlockSpec((CHUNK, wa), lambda c: (c, 3)),
        pl.BlockSpec((CHUNK, LANES), lambda c: (c, 0)),
        pl.BlockSpec((None, 1, kw - 1, 3 * wa), lambda c: (layer, sched.sample_seq(c), 0, 0)),
        pl.BlockSpec((None, 1, ha, dka, dva), lambda c: (layer, sched.sample_seq(c), 0, 0, 0)),
        pl.BlockSpec((kw, 3 * wa), lambda c: (0, 0)),
        pl.BlockSpec((1, LANES), lambda c: (0, 0)),
        pl.BlockSpec((1, LANES), lambda c: (0, 0)),
        pl.BlockSpec((1, dva), lambda c: (0, 0)),
    ]
    gdn_args = [pa, pa, pab, conv_init, gdn_init, conv_w, pad(a_log), pad(dt_bias), norm_w.reshape(1, dva)]
    rp_specs = [
        pl.BlockSpec((CHUNK, qw), lambda c: (c, 0)),
        pl.BlockSpec((CHUNK, qw), lambda c: (c, 1)),
        pl.BlockSpec((CHUNK, vw), lambda c: (c, 1)),
        pl.BlockSpec((CHUNK, vw), lambda c: (c, 2)),
        pl.BlockSpec((CHUNK, dkb), lambda c: (sched.pos_chunk(c), 0)),
        pl.BlockSpec((CHUNK, dkb), lambda c: (sched.pos_chunk(c), 0)),
        whole(dmat), whole(qdec), whole(ktail), whole(gblk),
        pl.BlockSpec((None, 1, hb, dkb, dvb), lambda c: (layer, sched.sample_seq(c), 0, 0, 0)),
        pl.BlockSpec((CHUNK, width), lambda c: (c, 0)),
        pl.BlockSpec((None, 1, nbuf, width), lambda c: (layer, sched.sample_seq(c), 0, 0)),
        pl.BlockSpec((n_grp, gc, gc), lambda c: (0, 0, 0)),
        pl.BlockSpec((1, width), lambda c: (0, 0)),
    ]
    rp_args = [pb, pb, pb, pb, cos2, sin2, dmat, qdec, ktail, gblk, ret_init, pc, buf_init, pool_w.astype(bf16),
               pool_scale.reshape(1, width)]
    assert len(gdn_specs) == N_GDN_IN and len(rp_specs) == N_RETPOOL_IN
    d_mix = wa + vw + width
    return pl.pallas_call(
        functools.partial(_mixer_kernel, sched=sched, gdn_cfg=dict(n_heads=ha, dk=dka, dv=dva),
                          ret_cfg=dict(n_heads=hb, dk=dkb, dv=dvb), gc=gc),
        grid=(sched.n_chunks,),
        in_specs=gdn_specs + rp_specs,
        out_specs=[
            pl.BlockSpec((CHUNK, d_mix), lambda c: (c, 0)),
            pl.BlockSpec((1, ha, dka, dva), lambda c: (seq(c), 0, 0, 0)),
            pl.BlockSpec((1, kw - 1, 3 * wa), lambda c: (seq(c), 0, 0)),
            pl.BlockSpec((1, hb, dkb, dvb), lambda c: (seq(c), 0, 0, 0)),
            pl.BlockSpec((1, nbuf, width), lambda c: (seq(c), 0, 0)),
        ],
        out_shape=[
            jax.ShapeDtypeStruct((m, d_mix), bf16),
            jax.ShapeDtypeStruct((n_seq, ha, dka, dva), f32),
            jax.ShapeDtypeStruct((n_seq, kw - 1, 3 * wa), f32),
            jax.ShapeDtypeStruct((n_seq, hb, dkb, dvb), f32),
            jax.ShapeDtypeStruct((n_seq, nbuf, width), f32),
        ],
        scratch_shapes=[
            pltpu.VMEM((8, 3 * wa), f32),
            pltpu.VMEM((ha, dka, dva), f32),
            pltpu.VMEM((LANES, 2 * CHUNK), f32),
            pltpu.VMEM((hb, dkb, dvb), f32),
            pltpu.VMEM((16, width), f32),
        ],
        compiler_params=_params(("arbitrary",)),
        name="mixers",
    )(*gdn_args, *rp_args)


def _retpool_kernel(q_ref, k_ref, v_ref, g_ref, cos_ref, sin_ref, dmat_ref, qdec_ref, ktail_ref, gblk_ref, sinit_ref,
                    u_ref, binit_ref, pw_ref, pscale_ref, o_ref, sout_ref, oc_ref, bout_ref, s_ref, hist_ref,
                    *, sched, n_heads, dk, dv, gc, phase):
    c = pl.program_id(0)
    in_prompt, _, ci, is_last = sched.split(c)
    hist = hist_ref.shape[0]
    nbuf = binit_ref.shape[1]

    if phase == "init":
        @pl.when((ci == 0) & in_prompt)
        def _():
            s_ref[...] = jnp.zeros(s_ref.shape, f32)
            hist_ref[...] = jnp.zeros(hist_ref.shape, f32)

        @pl.when((ci == 0) & jnp.logical_not(in_prompt))
        def _():
            s_ref[...] = sinit_ref[0]
            hist_ref[0:hist - nbuf, :] = jnp.zeros((hist - nbuf, hist_ref.shape[1]), f32)
            hist_ref[hist - nbuf:hist, :] = binit_ref[0]
        return

    if phase == "finish":
        @pl.when(is_last)
        def _():
            sout_ref[0] = s_ref[...]
            bout_ref[0] = u_ref[CHUNK - nbuf:CHUNK, :]
        return

    cos = cos_ref[...]
    sin = sin_ref[...]

    def rope(t):
        return t * cos + pltpu.roll(t, dk // 2, axis=1) * sin

    heads = range(n_heads)
    qr = [rope(q_ref[:, h * dk:(h + 1) * dk]) for h in heads]
    kr = [rope(k_ref[:, h * dk:(h + 1) * dk]) * (dk ** -0.5) for h in heads]
    vb = [v_ref[:, h * dv:(h + 1) * dv].astype(bf16) for h in heads]
    scores = [_dot_nt(qr[h].astype(bf16), kr[h].astype(bf16)) * dmat_ref[h] for h in heads]
    cross = [_dot((qr[h] * qdec_ref[h]).astype(bf16), s_ref[h].astype(bf16)) for h in heads]
    inner = [_dot(scores[h].astype(bf16), vb[h]) for h in heads]
    kv = [_dot_tn((kr[h] * ktail_ref[h]).astype(bf16), vb[h]) for h in heads]
    for h in heads:
        s_ref[h] = s_ref[h] * gblk_ref[h] + kv[h]
        o = inner[h] + cross[h]
        oc = o - jnp.mean(o, axis=-1, keepdims=True)
        on = oc * lax.rsqrt(jnp.mean(oc * oc, axis=-1, keepdims=True) + EPS)
        o_ref[:, h * dv:(h + 1) * dv] = (on * _silu(g_ref[:, h * dv:(h + 1) * dv])).astype(o_ref.dtype)

    pos = (jnp.where(in_prompt, 0, PAST_LEN) + ci * CHUNK
           + lax.broadcasted_iota(jnp.int32, (CHUNK, 1), 0))
    for gi, win_len in enumerate(POOL_WINDOWS):
        cols = slice(gi * gc, (gi + 1) * gc)
        ext = jnp.concatenate([hist_ref[:, cols], u_ref[:, cols]], axis=0)
        acc, span = ext, 1
        while span < win_len:
            acc = acc + pltpu.roll(acc, span, axis=0)
            span *= 2
        assert span == win_len and win_len - 1 <= hist
        cur = ext[hist:hist + CHUNK, :]
        cnt = jnp.minimum(pos + 1, win_len).astype(f32)
        d = acc[hist:hist + CHUNK, :] / cnt - cur
        y = _dot(d.astype(bf16), pw_ref[gi]) * pscale_ref[:, cols]
        oc_ref[:, cols] = y.astype(oc_ref.dtype)
    hist_ref[...] = u_ref[CHUNK - hist:CHUNK, :]


def _ret_tables(n_heads, dk, dv, n_pos):
    half = dk // 2
    inv_freq = ROPE_BASE ** (-jnp.arange(half, dtype=f32) / half)
    ang = jnp.arange(n_pos, dtype=jnp.int32).astype(f32)[:, None] * inv_freq[None, :]
    cos, sin = jnp.cos(ang), jnp.sin(ang)
    cos2 = jnp.concatenate([cos, cos], axis=-1)
    sin2 = jnp.concatenate([-sin, sin], axis=-1)
    lg = jnp.log1p(-jnp.exp2(-5.0 - jnp.arange(n_heads, dtype=f32)))
    idx = jnp.arange(CHUNK, dtype=f32)
    rel = idx[:, None] - idx[None, :]
    causal = rel >= 0
    dmat = jnp.where(causal, jnp.exp(jnp.where(causal, rel, 0.0) * lg[:, None, None]), 0.0)
    qdec = jnp.exp((idx + 1.0)[None, :] * lg[:, None])
    ktail = jnp.exp((CHUNK - 1.0 - idx)[None, :] * lg[:, None])
    gblk = jnp.exp(CHUNK * lg)
    bl = lambda t: jnp.broadcast_to(t[:, :, None], (n_heads, CHUNK, dk))
    return cos2, sin2, dmat, bl(qdec), bl(ktail), jnp.broadcast_to(gblk[:, None, None], (n_heads, 1, dv))


def kernel(x_prompt, x_sample, state_conv, state_gdn, state_ret, state_pool, ffn1_norm, ffn1_w1, ffn1_w3, ffn1_w2,
           mix_norm, w_in, conv_w, gdn_a_log, gdn_dt_bias, gdn_norm_w, pool_w, pool_scale, w_out, ffn2_norm,
           ffn2_w1, ffn2_w3, ffn2_w2, final_norm):
    bp, lp, d = x_prompt.shape
    bs, ls, _ = x_sample.shape
    depth = ffn1_w1.shape[0]
    sched = _Sched(bp, lp, bs, ls)
    mp, ms = bp * lp, bs * ls
    n_pos = max(lp, PAST_LEN + ls)

    _, _, ha, dka, dva = state_gdn.shape
    _, _, hb, dkb, dvb = state_ret.shape
    wa, qkb, wb = ha * dva, hb * dkb, hb * dvb
    pool_width = state_pool.shape[-1]
    in_sizes = (2 * ha * dka + wa, wa, ha, ha, qkb, qkb, wb, wb, pool_width)
    assert sum(in_sizes) == w_in.shape[-1]
    offs = [0]
    for s in in_sizes:
        offs.append(offs[-1] + s)

    f_hidden = ffn1_w1.shape[-1]
    f_pad = f_hidden + (-f_hidden % FFN_TILE)
    ffns = []
    for l in range(depth):
        ffns += [(ffn1_w1, ffn1_w3, ffn1_w2, l), (ffn2_w1, ffn2_w3, ffn2_w2, l)]
    up_job = lambda w, layer: _CastJob(w, layer, d, ((0, f_hidden, f_pad),))
    x, hq, ss, casts = _norm_prep(x_prompt.reshape(mp, d), x_sample.reshape(ms, d), ffn1_norm[0],
                                  (up_job(ffn1_w1, 0), up_job(ffn1_w3, 0)))
    up_w = [casts[0][0], casts[1][0]]
    span = lambda i, j, out=None: (offs[i], offs[j] - offs[i], out or offs[j] - offs[i])
    in_windows = (span(0, 2), span(2, 4, LANES), span(4, 8), span(8, 9))
    mix_w = {}

    def ffn_up_with_casts(k, hq, ss):
        w2s, layer = ffns[k][2], ffns[k][3]
        jobs = [_CastJob(w2s, layer, f_pad, ((0, d, d),))]
        if k + 1 < len(ffns):
            n1, n3, _, nl = ffns[k + 1]
            jobs += [up_job(n1, nl), up_job(n3, nl)]
        mix_layer = 0 if k == 0 else (k + 1) // 2 if k % 2 == 1 and (k + 1) // 2 < depth else None
        if mix_layer is not None:
            jobs += [_CastJob(w_in, mix_layer, d, in_windows),
                     _CastJob(w_out, mix_layer, w_out.shape[1], ((0, d, d),))]
        hmid, casts = _ffn_up(hq, ss, up_w[0], up_w[1], 0, 0.5, tuple(jobs))
        if k + 1 < len(ffns):
            up_w[:] = [casts[1][0], casts[2][0]]
        if mix_layer is not None:
            mix_w[mix_layer] = casts[-2] + casts[-1]
        return hmid, casts[0][0]

    ffn_rows = (512, 256, 128, 64)
    out_rows = (1024, 512, 256, 128, 64)

    outs = {k: [] for k in ("conv_p", "gdn_p", "ret_p", "pool_p", "conv_s", "gdn_s", "ret_s", "pool_s")}
    for l in range(depth):
        hmid, w2 = ffn_up_with_casts(2 * l, hq, ss)
        x, hq, ss = _proj_residual([hmid], w2, 0, x, mix_norm[l], ffn_rows)
        w_a, w_ab, w_b, w_c, w_o = mix_w[l]
        pa, pab, pb, pc = (_in_proj(hq, ss, w, 0) for w in (w_a, w_ab, w_b, w_c))
        mix, sa, ca, sb, pbuf = _mixers(pa, pab, pb, pc, state_conv, state_gdn, state_ret, state_pool, l, conv_w[l],
                                        gdn_a_log[l], gdn_dt_bias[l], gdn_norm_w[l], pool_w[l], pool_scale[l],
                                        sched, n_pos)
        x, hq, ss = _proj_residual([mix], w_o, 0, x, ffn2_norm[l], out_rows)
        hmid, w2 = ffn_up_with_casts(2 * l + 1, hq, ss)
        if l + 1 < depth:
            x, hq, ss = _proj_residual([hmid], w2, 0, x, ffn1_norm[l + 1], ffn_rows)
        else:
            x = _proj_residual([hmid], w2, 0, x, None, ffn_rows)
        for k, v in (("conv", ca), ("pool", pbuf), ("gdn", sa), ("ret", sb)):
            outs[k + "_p"].append(v[:bp])
            outs[k + "_s"].append(v[bp:])

    y_prompt = _final_norm(x, final_norm, 0, mp).reshape(bp, lp, d)
    y_sample = _final_norm(x, final_norm, mp, ms).reshape(bs, ls, d)
    st = {k: jnp.stack(v) for k, v in outs.items()}
    return (y_prompt, y_sample, st["conv_p"], st["gdn_p"], st["ret_p"], st["pool_p"], st["conv_s"], st["gdn_s"],
            st["ret_s"], st["pool_s"])
```

```python
import functools
import math
from typing import NamedTuple

import jax
import jax.numpy as jnp
from jax import lax
from jax.experimental import pallas as pl
from jax.experimental.pallas import tpu as pltpu

CHUNK = 64
EPS = 1e-6
ROPE_BASE = 10000.0
POOL_WINDOWS = (2, 4, 8, 16)
PAST_LEN = 4096
LANES = 128
GDN_HEAD_GROUP = 12
FFN_TILE = 512
VMEM_LIMIT_BYTES = 56 * 1024 * 1024

f32 = jnp.float32
bf16 = jnp.bfloat16
HI = lax.Precision.HIGHEST


def _pick(n, candidates):
    for c in candidates:
        if n % c == 0:
            return c
    raise ValueError(f"no block size in {candidates} divides {n}")


def _params(sem):
    return pltpu.CompilerParams(dimension_semantics=sem, vmem_limit_bytes=VMEM_LIMIT_BYTES)


def _silu(x):
    return x * jax.nn.sigmoid(x)


def _dot(a, b):
    return jnp.dot(a, b, preferred_element_type=f32)


def _dot_hi(a, b):
    return jnp.dot(a, b, preferred_element_type=f32, precision=HI)


def _dot_nt(a, b):
    return lax.dot_general(a, b, (((1,), (1,)), ((), ())), preferred_element_type=f32)


def _dot_tn(a, b):
    return lax.dot_general(a, b, (((0,), (0,)), ((), ())), preferred_element_type=f32)


def _row_rsqrt(ss_ref, d_model):
    return lax.rsqrt(ss_ref[:, 0:1] / d_model + EPS)


def _norm_prep_kernel(*refs, na, job_dims):
    n_jobs = len(job_dims)
    xa_ref, xb_ref, g_ref = refs[:3]
    src_refs = refs[3:3 + n_jobs]
    x_ref, hq_ref, ss_ref = refs[3 + n_jobs:6 + n_jobs]
    dst_refs = refs[6 + n_jobs:]

    def emit(src_ref):
        x = src_ref[...]
        x_ref[...] = x
        hq_ref[...] = (x * g_ref[...]).astype(hq_ref.dtype)
        ss_ref[...] = jnp.broadcast_to(jnp.sum(x * x, axis=-1, keepdims=True), ss_ref.shape)

    i = pl.program_id(0)
    pl.when(i < na)(lambda: emit(xa_ref))
    pl.when(i >= na)(lambda: emit(xb_ref))
    _run_jobs(job_dims, src_refs, dst_refs, i)


def _norm_prep(xa, xb, g, jobs=()):
    (ma, d), mb = xa.shape, xb.shape[0]
    bm = _pick(math.gcd(ma, mb), (256, 128, 64))
    na, m = ma // bm, ma + mb
    job_in, job_out, job_shape, job_dims = _job_plan(jobs, m // bm, lambda i: i)
    row = pl.BlockSpec((bm, d), lambda i: (i, 0))
    res = pl.pallas_call(
        functools.partial(_norm_prep_kernel, na=na, job_dims=job_dims),
        grid=(m // bm,),
        in_specs=[
            pl.BlockSpec((bm, d), lambda i: (jnp.minimum(i, na - 1), 0)),
            pl.BlockSpec((bm, d), lambda i: (jnp.maximum(i - na, 0), 0)),
            pl.BlockSpec((1, d), lambda i: (0, 0)),
        ] + job_in,
        out_specs=[row, row, pl.BlockSpec((bm, LANES), lambda i: (i, 0))] + job_out,
        out_shape=[jax.ShapeDtypeStruct((m, d), f32), jax.ShapeDtypeStruct((m, d), bf16),
                   jax.ShapeDtypeStruct((m, LANES), f32)] + job_shape,
        compiler_params=_params(("arbitrary",)),
        name="norm_prep",
    )(xa, xb, g.reshape(1, d), *[job.src for job in jobs])
    return res[0], res[1], res[2], _split_copies(jobs, res[3:])


def _final_norm_kernel(x_ref, g_ref, o_ref):
    x = x_ref[...]
    o_ref[...] = x * lax.rsqrt(jnp.mean(x * x, axis=-1, keepdims=True) + EPS) * g_ref[...]


def _final_norm(x, g, row0, n_rows):
    d = x.shape[1]
    bm = _pick(n_rows, (512, 256, 128, 64))
    assert row0 % bm == 0
    b0 = row0 // bm
    return pl.pallas_call(
        _final_norm_kernel,
        grid=(n_rows // bm,),
        in_specs=[pl.BlockSpec((bm, d), lambda i: (i + b0, 0)), pl.BlockSpec((1, d), lambda i: (0, 0))],
        out_specs=pl.BlockSpec((bm, d), lambda i: (i, 0)),
        out_shape=jax.ShapeDtypeStruct((n_rows, d), f32),
        compiler_params=_params(("parallel",)),
        name="final_norm",
    )(x, g.reshape(1, d))


class _CastJob(NamedTuple):
    src: jax.Array
    layer: int
    rows_out: int
    windows: tuple


def _job_tile_rows(job, n_steps):
    rows = job.src.shape[1]
    for tr in (16, 32, 64, 128, 256, 512, 1024):
        if rows % tr == 0 and job.rows_out % tr == 0 and job.rows_out // tr <= n_steps:
            return tr
    raise ValueError("cast job does not fit the host grid")


def _job_plan(jobs, n_steps, step_of):
    in_specs, out_specs, out_shape, dims = [], [], [], []
    for job in jobs:
        tr = _job_tile_rows(job, n_steps)
        n_src, n_all = job.src.shape[1] // tr, job.rows_out // tr
        in_specs.append(pl.BlockSpec(
            (None, tr, job.src.shape[2]),
            lambda *g, n_src=n_src, layer=job.layer: (layer, jnp.minimum(step_of(*g), n_src - 1), 0)))
        for _, _, cols_out in job.windows:
            out_specs.append(pl.BlockSpec(
                (tr, cols_out), lambda *g, n_all=n_all: (jnp.minimum(step_of(*g), n_all - 1), 0)))
            out_shape.append(jax.ShapeDtypeStruct((job.rows_out, cols_out), bf16))
        dims.append((tr, job.src.shape[1], job.rows_out, job.windows))
    return in_specs, out_specs, out_shape, tuple(dims)


def _run_jobs(job_dims, src_refs, dst_refs, t):
    dst_refs = list(dst_refs)
    for (tr, rows, rows_out, windows), src_ref in zip(job_dims, src_refs):
        for col0, cols, cols_out in windows:
            dst_ref = dst_refs.pop(0)
            val = src_ref[:, col0:col0 + cols].astype(bf16)
            if rows_out > rows:
                val = jnp.where(t < rows // tr, val, jnp.zeros_like(val))
            dst_ref[:, 0:cols] = val
            if cols_out > cols:
                dst_ref[:, cols:cols_out] = jnp.zeros((tr, cols_out - cols), bf16)


def _split_copies(jobs, outs):
    copies, k = [], 0
    for job in jobs:
        copies.append(list(outs[k:k + len(job.windows)]))
        k += len(job.windows)
    return copies


def _up_kernel(*refs, d_model, scale, job_dims, nj, last_cols):
    n_jobs = len(job_dims)
    h_ref, ss_ref, w1_ref, w3_ref = refs[:4]
    src_refs, o_ref, dst_refs = refs[4:4 + n_jobs], refs[4 + n_jobs], refs[5 + n_jobs:]
    bm, bn = o_ref.shape

    def tile(cols):
        h = h_ref[...]
        r = _row_rsqrt(ss_ref, d_model)
        g = _dot(h, w1_ref[:, 0:cols]) * r
        u = _dot(h, w3_ref[:, 0:cols]) * r
        o_ref[:, 0:cols] = (scale * _silu(g) * u).astype(o_ref.dtype)
        if cols < bn:
            o_ref[:, cols:bn] = jnp.zeros((bm, bn - cols), o_ref.dtype)

    if last_cols == bn:
        tile(bn)
    else:
        j = pl.program_id(1)
        pl.when(j < nj - 1)(lambda: tile(bn))
        pl.when(j == nj - 1)(lambda: tile(last_cols))
    _run_jobs(job_dims, src_refs, dst_refs, pl.program_id(0) * nj + pl.program_id(1))


def _wspec(w, layer, row0, rows, bn):
    assert row0 % rows == 0
    if w.ndim == 2:
        return pl.BlockSpec((rows, bn), lambda i, j: (row0 // rows, j))
    return pl.BlockSpec((None, rows, bn), lambda i, j: (layer, row0 // rows, j))


def _ffn_up(hq, ss, w1, w3, layer, f_valid, scale, jobs=()):
    m, d = hq.shape
    f = w1.shape[-1]
    bm = _pick(m, (1024, 512, 256, 128, 64))
    bn = _pick(f, (FFN_TILE, 256, 128))
    ni, nj = m // bm, f // bn
    last_cols = f_valid - (nj - 1) * bn
    assert 0 < last_cols <= bn and last_cols % LANES == 0
    job_in, job_out, job_shape, job_dims = _job_plan(jobs, ni * nj, lambda i, j: i * nj + j)
    in_specs = [
        pl.BlockSpec((bm, d), lambda i, j: (i, 0)),
        pl.BlockSpec((bm, LANES), lambda i, j: (i, 0)),
        _wspec(w1, layer, 0, d, bn),
        _wspec(w3, layer, 0, d, bn),
    ]
    res = pl.pallas_call(
        functools.partial(_up_kernel, d_model=d, scale=scale, job_dims=job_dims, nj=nj, last_cols=last_cols),
        grid=(ni, nj),
        in_specs=in_specs + job_in,
        out_specs=[pl.BlockSpec((bm, bn), lambda i, j: (i, j))] + job_out,
        out_shape=[jax.ShapeDtypeStruct((m, f), bf16)] + job_shape,
        compiler_params=_params(("arbitrary", "arbitrary")),
        name="ffn_up",
    )(hq, ss, w1, w3, *[job.src for job in jobs])
    return res[0], _split_copies(jobs, res[1:])


def _in_proj_kernel(h_ref, ss_ref, w_ref, o_ref, *, d_model):
    o_ref[...] = _dot(h_ref[...], w_ref[...]) * _row_rsqrt(ss_ref, d_model)


def _in_proj(hq, ss, w, layer):
    m, d = hq.shape
    n = w.shape[-1]
    bm = _pick(m, (1024, 512, 256, 128, 64))
    bn = _pick(n, (1024, 768, 512, 384, 256, 128))
    return pl.pallas_call(
        functools.partial(_in_proj_kernel, d_model=d),
        grid=(m // bm, n // bn),
        in_specs=[
            pl.BlockSpec((bm, d), lambda i, j: (i, 0)),
            pl.BlockSpec((bm, LANES), lambda i, j: (i, 0)),
            _wspec(w, layer, 0, d, bn),
        ],
        out_specs=pl.BlockSpec((bm, bn), lambda i, j: (i, j)),
        out_shape=jax.ShapeDtypeStruct((m, n), f32),
        compiler_params=_params(("parallel", "arbitrary")),
        name="in_proj",
    )(hq, ss, w)


def _proj_residual_kernel(a_ref, w_ref, x_ref, *rest, with_norm):
    xn = x_ref[...] + _dot(a_ref[:, 0:w_ref.shape[0]], w_ref[...])
    if not with_norm:
        rest[0][...] = xn
        return
    g_ref, xo_ref, hq_ref, ss_ref = rest
    xo_ref[...] = xn
    hq_ref[...] = (xn * g_ref[...]).astype(hq_ref.dtype)
    part = jnp.broadcast_to(jnp.sum(xn * xn, axis=-1, keepdims=True), ss_ref.shape)
    j = pl.program_id(1)

    @pl.when(j == 0)
    def _():
        ss_ref[...] = part

    @pl.when(j != 0)
    def _():
        ss_ref[...] += part


def _proj_residual(a, w, layer, x, g, bm_cands):
    m, d = x.shape
    bm = _pick(m, bm_cands)
    bn = _pick(d, (512, 256, 128))
    assert w.shape[-1] == d and w.shape[-2] <= a.shape[1] and w.shape[-2] % LANES == 0
    with_norm = g is not None
    tile = pl.BlockSpec((bm, bn), lambda i, j: (i, j))
    in_specs = [pl.BlockSpec((bm, a.shape[1]), lambda i, j: (i, 0)), _wspec(w, layer, 0, w.shape[-2], bn), tile]
    args = [a, w, x]
    if with_norm:
        in_specs.append(pl.BlockSpec((1, bn), lambda i, j: (0, j)))
        args.append(g.reshape(1, d))
        out_specs = [tile, tile, pl.BlockSpec((bm, LANES), lambda i, j: (i, 0))]
        out_shape = [jax.ShapeDtypeStruct((m, d), f32), jax.ShapeDtypeStruct((m, d), bf16),
                     jax.ShapeDtypeStruct((m, LANES), f32)]
    else:
        out_specs, out_shape = tile, jax.ShapeDtypeStruct((m, d), f32)
    return pl.pallas_call(
        functools.partial(_proj_residual_kernel, with_norm=with_norm),
        grid=(m // bm, d // bn),
        in_specs=in_specs,
        out_specs=out_specs,
        out_shape=out_shape,
        compiler_params=_params(("parallel", "arbitrary")),
        name="proj_residual",
    )(*args)


class _Sched:
    def __init__(self, bp, lp, bs, ls):
        assert lp % CHUNK == 0 and ls % CHUNK == 0 and PAST_LEN % CHUNK == 0
        self.bp, self.np_, self.bs, self.ns = bp, lp // CHUNK, bs, ls // CHUNK
        self.n_prompt = self.bp * self.np_
        self.n_chunks = self.n_prompt + self.bs * self.ns
        self.n_seq = bp + bs

    def split(self, c):
        in_prompt = c < self.n_prompt
        cs = jnp.maximum(c - self.n_prompt, 0)
        seq = jnp.where(in_prompt, c // self.np_, self.bp + cs // self.ns)
        ci = jnp.where(in_prompt, c % self.np_, cs % self.ns)
        last = jnp.where(in_prompt, self.np_ - 1, self.ns - 1)
        return in_prompt, seq, ci, ci == last

    def seq(self, c):
        return self.split(c)[1]

    def sample_seq(self, c):
        return jnp.maximum(self.seq(c) - self.bp, 0)

    def pos_chunk(self, c):
        in_prompt, _, ci, _ = self.split(c)
        return jnp.where(in_prompt, 0, PAST_LEN // CHUNK) + ci


def _strict_lower_inverse_minus_eye(lows, row, col):
    pair = (row // 2) == (col // 2)
    es = [-jnp.where(pair, low, 0.0) for low in lows]
    b = 2
    while b < CHUNK:
        mask = ((row // (2 * b)) == (col // (2 * b))) & ((row // b) != (col // b))
        cs = [jnp.where(mask, low, 0.0) for low in lows]
        xs = [c + _dot(e.astype(bf16), c.astype(bf16)) for e, c in zip(es, cs)]
        ys = [x + _dot(x.astype(bf16), e.astype(bf16)) for x, e in zip(xs, es)]
        es = [e - y for e, y in zip(es, ys)]
        b *= 2
    return es


def _gdn_kernel(qkv_ref, z_ref, ab_ref, cinit_ref, sinit_ref, cw_ref, alog_ref, dtb_ref, nw_ref,
                o_ref, sout_ref, cout_ref, hist_ref, s_ref, gt_ref, *, sched, n_heads, dk, dv, phase):
    c = pl.program_id(0)
    in_prompt, _, ci, is_last = sched.split(c)
    w = n_heads * dk
    hist = hist_ref.shape[0]
    kw = cw_ref.shape[0]

    if phase == "init":
        @pl.when((ci == 0) & in_prompt)
        def _():
            hist_ref[...] = jnp.zeros(hist_ref.shape, f32)
            s_ref[...] = jnp.zeros(s_ref.shape, f32)

        @pl.when((ci == 0) & jnp.logical_not(in_prompt))
        def _():
            hist_ref[0:hist - (kw - 1), :] = jnp.zeros((hist - (kw - 1), hist_ref.shape[1]), f32)
            hist_ref[hist - (kw - 1):hist, :] = cinit_ref[0]
            s_ref[...] = sinit_ref[0]
        return

    if phase == "finish":
        @pl.when(is_last)
        def _():
            sout_ref[0] = s_ref[...]
            cout_ref[0] = qkv_ref[CHUNK - (kw - 1):CHUNK, :]
        return

    ab = ab_ref[...]
    sp = jnp.maximum(ab + dtb_ref[...], 0.0) + jnp.log1p(jnp.exp(-jnp.abs(ab + dtb_ref[...])))
    g_all = -jnp.exp(alog_ref[...]) * sp
    beta_all = jax.nn.sigmoid(ab)

    r2 = lax.broadcasted_iota(jnp.int32, (2 * CHUNK, CHUNK), 0)
    c2 = lax.broadcasted_iota(jnp.int32, (2 * CHUNK, CHUNK), 1)
    gp = _dot_hi((c2 <= r2).astype(f32), g_all)
    gt_ref[...] = gp.T
    g_cum = gp[0:CHUNK, :]
    eg_all = jnp.exp(g_cum)
    g_last = g_cum[CHUNK - 1:CHUNK, :]
    tail_all = jnp.exp(g_last - g_cum)
    gblk_all = jnp.exp(g_last)

    row = lax.broadcasted_iota(jnp.int32, (CHUNK, CHUNK), 0)
    col = lax.broadcasted_iota(jnp.int32, (CHUNK, CHUNK), 1)
    causal = row >= col
    strict = row > col

    def conv(off):
        xt = jnp.concatenate([hist_ref[:, off:off + dk], qkv_ref[:, off:off + dk]], axis=0)
        acc = xt * cw_ref[0:1, off:off + dk]
        for i in range(1, kw):
            acc = pltpu.roll(acc, 1, axis=0) + xt * cw_ref[i:i + 1, off:off + dk]
        return _silu(acc[hist:hist + CHUNK, :])

    def l2n(t, scale=1.0):
        return t * (lax.rsqrt(jnp.sum(t * t, axis=-1, keepdims=True) + EPS) * scale)

    for g0 in range(0, n_heads, GDN_HEAD_GROUP):
        heads = range(g0, min(g0 + GDN_HEAD_GROUP, n_heads))
        lows, rhs_k, ktail_b, qdec_b, qkd_b, bc = [], {}, {}, {}, {}, {}
        for h in heads:
            bc[h] = beta_all[:, n_heads + h:n_heads + h + 1]
            egc = eg_all[:, h:h + 1]
            kn = l2n(conv(w + h * dk))
            kb = kn.astype(bf16)
            kk = _dot_nt(kb, kb)
            qn = l2n(conv(h * dk), dk ** -0.5)
            qk = _dot_nt(qn.astype(bf16), kb)
            decay = jnp.where(causal, jnp.exp(jnp.where(
                causal, g_cum[:, h:h + 1] - gt_ref[h:h + 1, 0:CHUNK], 0.0)), 0.0)
            lows.append(jnp.where(strict, kk * decay * bc[h], 0.0))
            qkd_b[h] = (qk * decay).astype(bf16)
            qdec_b[h] = (qn * egc).astype(bf16)
            ktail_b[h] = (kn * tail_all[:, h:h + 1]).astype(bf16)
            rhs_k[h] = kn * (bc[h] * egc)
        es = dict(zip(heads, _strict_lower_inverse_minus_eye(lows, row, col)))
        rhs = {h: jnp.concatenate([conv(2 * w + h * dv) * bc[h], rhs_k[h]], axis=-1) for h in heads}
        sol = {h: rhs[h] + _dot(es[h].astype(bf16), rhs[h].astype(bf16)) for h in heads}

        sb = {h: s_ref[h].astype(bf16) for h in heads}
        ws = {h: _dot(sol[h][:, dv:dv + dk].astype(bf16), sb[h]) for h in heads}
        ub = {h: (sol[h][:, 0:dv] - ws[h]).astype(bf16) for h in heads}
        qs = {h: _dot(qdec_b[h], sb[h]) for h in heads}
        qu = {h: _dot(qkd_b[h], ub[h]) for h in heads}
        ku = {h: _dot_tn(ktail_b[h], ub[h]) for h in heads}
        for h in heads:
            s_ref[h] = s_ref[h] * gblk_all[:, h:h + 1] + ku[h]
            o = qs[h] + qu[h]
            o = o * lax.rsqrt(jnp.mean(o * o, axis=-1, keepdims=True) + EPS) * nw_ref[...]
            o = o * _silu(z_ref[:, h * dv:(h + 1) * dv])
            o_ref[:, h * dv:(h + 1) * dv] = o.astype(o_ref.dtype)
    hist_ref[...] = qkv_ref[CHUNK - hist:CHUNK, :]


N_GDN_IN, N_RETPOOL_IN = 9, 15


def _mixer_kernel(*refs, sched, gdn_cfg, ret_cfg, gc):
    gdn_in = refs[:N_GDN_IN]
    rp_in = refs[N_GDN_IN:N_GDN_IN + N_RETPOOL_IN]
    mix_ref, sa_ref, cout_ref, sb_ref, bout_ref, xp_ref, s_a, gt_ref, s_b, up_ref = refs[N_GDN_IN + N_RETPOOL_IN:]
    wa = gdn_cfg["n_heads"] * gdn_cfg["dv"]
    wb = ret_cfg["n_heads"] * ret_cfg["dv"]
    wc = mix_ref.shape[1] - wa - wb
    for phase in ("init", "body", "finish"):
        _gdn_kernel(*gdn_in, mix_ref.at[:, pl.ds(0, wa)], sa_ref, cout_ref, xp_ref, s_a, gt_ref, sched=sched,
                    phase=phase, **gdn_cfg)
        _retpool_kernel(*rp_in, mix_ref.at[:, pl.ds(wa, wb)], sb_ref, mix_ref.at[:, pl.ds(wa + wb, wc)], bout_ref,
                        s_b, up_ref, sched=sched, gc=gc, phase=phase, **ret_cfg)


def _mixers(pa, pab, pb, pc, conv_init, gdn_init, ret_init, buf_init, layer, conv_w, a_log, dt_bias, norm_w,
            pool_w, pool_scale, sched, n_pos):
    m, width = pc.shape
    _, _, ha, dka, dva = gdn_init.shape
    _, _, hb, dkb, dvb = ret_init.shape
    n_seq = sched.n_seq
    assert gdn_init.shape[1] == sched.bs
    n_grp, gc, _ = pool_w.shape
    nbuf = buf_init.shape[2]
    kw = conv_w.shape[0]
    assert dka == dva == LANES and 2 * ha <= LANES
    assert dkb == LANES and dvb == 2 * dkb
    assert n_grp == len(POOL_WINDOWS) and nbuf == max(POOL_WINDOWS) - 1 and gc % LANES == 0
    wa, qw, vw = ha * dka, hb * dkb, hb * dvb
    pad = lambda t: jnp.pad(t.reshape(1, -1), ((0, 0), (0, LANES - t.size)))
    cos2, sin2, dmat, qdec, ktail, gblk = _ret_tables(hb, dkb, dvb, n_pos)
    whole = lambda a: pl.BlockSpec(a.shape, lambda c: (0,) * a.ndim)
    seq = sched.seq
    gdn_specs = [
        pl.BlockSpec((CHUNK, 3 * wa), lambda c: (c, 0)),
        pl.BlockSpec((CHUNK, wa), lambda c: (c, 3)),
        pl.BlockSpec((CHUNK, LANES), lambda c: (c, 0)),
        pl.BlockSpec((None, 1, kw - 1, 3 * wa), lambda c: (layer, sched.sample_seq(c), 0, 0)),
        pl.BlockSpec((None, 1, ha, dka, dva), lambda c: (layer, sched.sample_seq(c), 0, 0, 0)),
        pl.BlockSpec((kw, 3 * wa), lambda c: (0, 0)),
        pl.BlockSpec((1, LANES), lambda c: (0, 0)),
        pl.BlockSpec((1, LANES), lambda c: (0, 0)),
        pl.BlockSpec((1, dva), lambda c: (0, 0)),
    ]
    gdn_args = [pa, pa, pab, conv_init, gdn_init, conv_w, pad(a_log), pad(dt_bias), norm_w.reshape(1, dva)]
    rp_specs = [
        pl.BlockSpec((CHUNK, qw), lambda c: (c, 0)),
        pl.BlockSpec((CHUNK, qw), lambda c: (c, 1)),
        pl.BlockSpec((CHUNK, vw), lambda c: (c, 1)),
        pl.BlockSpec((CHUNK, vw), lambda c: (c, 2)),
        pl.BlockSpec((CHUNK, dkb), lambda c: (sched.pos_chunk(c), 0)),
        pl.BlockSpec((CHUNK, dkb), lambda c: (sched.pos_chunk(c), 0)),
        whole(dmat), whole(qdec), whole(ktail), whole(gblk),
        pl.BlockSpec((None, 1, hb, dkb, dvb), lambda c: (layer, sched.sample_seq(c), 0, 0, 0)),
        pl.BlockSpec((CHUNK, width), lambda c: (c, 0)),
        pl.BlockSpec((None, 1, nbuf, width), lambda c: (layer, sched.sample_seq(c), 0, 0)),
        pl.BlockSpec((n_grp, gc, gc), lambda c: (0, 0, 0)),
        pl.BlockSpec((1, width), lambda c: (0, 0)),
    ]
    rp_args = [pb, pb, pb, pb, cos2, sin2, dmat, qdec, ktail, gblk, ret_init, pc, buf_init, pool_w.astype(bf16),
               pool_scale.reshape(1, width)]
    assert len(gdn_specs) == N_GDN_IN and len(rp_specs) == N_RETPOOL_IN
    d_mix = wa + vw + width
    return pl.pallas_call(
        functools.partial(_mixer_kernel, sched=sched, gdn_cfg=dict(n_heads=ha, dk=dka, dv=dva),
                          ret_cfg=dict(n_heads=hb, dk=dkb, dv=dvb), gc=gc),
        grid=(sched.n_chunks,),
        in_specs=gdn_specs + rp_specs,
        out_specs=[
            pl.BlockSpec((CHUNK, d_mix), lambda c: (c, 0)),
            pl.BlockSpec((1, ha, dka, dva), lambda c: (seq(c), 0, 0, 0)),
            pl.BlockSpec((1, kw - 1, 3 * wa), lambda c: (seq(c), 0, 0)),
            pl.BlockSpec((1, hb, dkb, dvb), lambda c: (seq(c), 0, 0, 0)),
            pl.BlockSpec((1, nbuf, width), lambda c: (seq(c), 0, 0)),
        ],
        out_shape=[
            jax.ShapeDtypeStruct((m, d_mix), bf16),
            jax.ShapeDtypeStruct((n_seq, ha, dka, dva), f32),
            jax.ShapeDtypeStruct((n_seq, kw - 1, 3 * wa), f32),
            jax.ShapeDtypeStruct((n_seq, hb, dkb, dvb), f32),
            jax.ShapeDtypeStruct((n_seq, nbuf, width), f32),
        ],
        scratch_shapes=[
            pltpu.VMEM((8, 3 * wa), f32),
            pltpu.VMEM((ha, dka, dva), f32),
            pltpu.VMEM((LANES, 2 * CHUNK), f32),
            pltpu.VMEM((hb, dkb, dvb), f32),
            pltpu.VMEM((16, width), f32),
        ],
        compiler_params=_params(("arbitrary",)),
        name="mixers",
    )(*gdn_args, *rp_args)


def _retpool_kernel(q_ref, k_ref, v_ref, g_ref, cos_ref, sin_ref, dmat_ref, qdec_ref, ktail_ref, gblk_ref, sinit_ref,
                    u_ref, binit_ref, pw_ref, pscale_ref, o_ref, sout_ref, oc_ref, bout_ref, s_ref, hist_ref,
                    *, sched, n_heads, dk, dv, gc, phase):
    c = pl.program_id(0)
    in_prompt, _, ci, is_last = sched.split(c)
    hist = hist_ref.shape[0]
    nbuf = binit_ref.shape[1]

    if phase == "init":
        @pl.when((ci == 0) & in_prompt)
        def _():
            s_ref[...] = jnp.zeros(s_ref.shape, f32)
            hist_ref[...] = jnp.zeros(hist_ref.shape, f32)

        @pl.when((ci == 0) & jnp.logical_not(in_prompt))
        def _():
            s_ref[...] = sinit_ref[0]
            hist_ref[0:hist - nbuf, :] = jnp.zeros((hist - nbuf, hist_ref.shape[1]), f32)
            hist_ref[hist - nbuf:hist, :] = binit_ref[0]
        return

    if phase == "finish":
        @pl.when(is_last)
        def _():
            sout_ref[0] = s_ref[...]
            bout_ref[0] = u_ref[CHUNK - nbuf:CHUNK, :]
        return

    cos = cos_ref[...]
    sin = sin_ref[...]

    def rope(t):
        return t * cos + pltpu.roll(t, dk // 2, axis=1) * sin

    heads = range(n_heads)
    qr = [rope(q_ref[:, h * dk:(h + 1) * dk]) for h in heads]
    kr = [rope(k_ref[:, h * dk:(h + 1) * dk]) * (dk ** -0.5) for h in heads]
    vb = [v_ref[:, h * dv:(h + 1) * dv].astype(bf16) for h in heads]
    scores = [_dot_nt(qr[h].astype(bf16), kr[h].astype(bf16)) * dmat_ref[h] for h in heads]
    cross = [_dot((qr[h] * qdec_ref[h]).astype(bf16), s_ref[h].astype(bf16)) for h in heads]
    inner = [_dot(scores[h].astype(bf16), vb[h]) for h in heads]
    kv = [_dot_tn((kr[h] * ktail_ref[h]).astype(bf16), vb[h]) for h in heads]
    for h in heads:
        s_ref[h] = s_ref[h] * gblk_ref[h] + kv[h]
        o = inner[h] + cross[h]
        oc = o - jnp.mean(o, axis=-1, keepdims=True)
        on = oc * lax.rsqrt(jnp.mean(oc * oc, axis=-1, keepdims=True) + EPS)
        o_ref[:, h * dv:(h + 1) * dv] = (on * _silu(g_ref[:, h * dv:(h + 1) * dv])).astype(o_ref.dtype)

    pos = (jnp.where(in_prompt, 0, PAST_LEN) + ci * CHUNK
           + lax.broadcasted_iota(jnp.int32, (CHUNK, 1), 0))
    for gi, win_len in enumerate(POOL_WINDOWS):
        cols = slice(gi * gc, (gi + 1) * gc)
        ext = jnp.concatenate([hist_ref[:, cols], u_ref[:, cols]], axis=0)
        acc, span = ext, 1
        while span < win_len:
            acc = acc + pltpu.roll(acc, span, axis=0)
            span *= 2
        assert span == win_len and win_len - 1 <= hist
        cur = ext[hist:hist + CHUNK, :]
        cnt = jnp.minimum(pos + 1, win_len).astype(f32)
        d = acc[hist:hist + CHUNK, :] / cnt - cur
        y = _dot(d.astype(bf16), pw_ref[gi]) * pscale_ref[:, cols]
        oc_ref[:, cols] = y.astype(oc_ref.dtype)
    hist_ref[...] = u_ref[CHUNK - hist:CHUNK, :]


def _ret_tables(n_heads, dk, dv, n_pos):
    half = dk // 2
    inv_freq = ROPE_BASE ** (-jnp.arange(half, dtype=f32) / half)
    ang = jnp.arange(n_pos, dtype=jnp.int32).astype(f32)[:, None] * inv_freq[None, :]
    cos, sin = jnp.cos(ang), jnp.sin(ang)
    cos2 = jnp.concatenate([cos, cos], axis=-1)
    sin2 = jnp.concatenate([-sin, sin], axis=-1)
    lg = jnp.log1p(-jnp.exp2(-5.0 - jnp.arange(n_heads, dtype=f32)))
    idx = jnp.arange(CHUNK, dtype=f32)
    rel = idx[:, None] - idx[None, :]
    causal = rel >= 0
    dmat = jnp.where(causal, jnp.exp(jnp.where(causal, rel, 0.0) * lg[:, None, None]), 0.0)
    qdec = jnp.exp((idx + 1.0)[None, :] * lg[:, None])
    ktail = jnp.exp((CHUNK - 1.0 - idx)[None, :] * lg[:, None])
    gblk = jnp.exp(CHUNK * lg)
    bl = lambda t: jnp.broadcast_to(t[:, :, None], (n_heads, CHUNK, dk))
    return cos2, sin2, dmat, bl(qdec), bl(ktail), jnp.broadcast_to(gblk[:, None, None], (n_heads, 1, dv))


def kernel(x_prompt, x_sample, state_conv, state_gdn, state_ret, state_pool, ffn1_norm, ffn1_w1, ffn1_w3, ffn1_w2,
           mix_norm, w_in, conv_w, gdn_a_log, gdn_dt_bias, gdn_norm_w, pool_w, pool_scale, w_out, ffn2_norm,
           ffn2_w1, ffn2_w3, ffn2_w2, final_norm):
    bp, lp, d = x_prompt.shape
    bs, ls, _ = x_sample.shape
    depth = ffn1_w1.shape[0]
    sched = _Sched(bp, lp, bs, ls)
    mp, ms = bp * lp, bs * ls
    n_pos = max(lp, PAST_LEN + ls)

    _, _, ha, dka, dva = state_gdn.shape
    _, _, hb, dkb, dvb = state_ret.shape
    wa, qkb, wb = ha * dva, hb * dkb, hb * dvb
    pool_width = state_pool.shape[-1]
    in_sizes = (2 * ha * dka + wa, wa, ha, ha, qkb, qkb, wb, wb, pool_width)
    assert sum(in_sizes) == w_in.shape[-1]
    offs = [0]
    for s in in_sizes:
        offs.append(offs[-1] + s)

    f_hidden = ffn1_w1.shape[-1]
    f_pad = f_hidden + (-f_hidden % FFN_TILE)
    ffns = []
    for l in range(depth):
        ffns += [(ffn1_w1, ffn1_w3, ffn1_w2, l), (ffn2_w1, ffn2_w3, ffn2_w2, l)]
    up_job = lambda w, layer: _CastJob(w, layer, d, ((0, f_hidden, f_pad),))
    x, hq, ss, casts = _norm_prep(x_prompt.reshape(mp, d), x_sample.reshape(ms, d), ffn1_norm[0],
                                  (up_job(ffn1_w1, 0), up_job(ffn1_w3, 0)))
    up_w = [casts[0][0], casts[1][0]]
    span = lambda i, j, out=None: (offs[i], offs[j] - offs[i], out or offs[j] - offs[i])
    in_windows = (span(0, 2), span(2, 4, LANES), span(4, 8), span(8, 9))
    mix_w = {}

    def ffn_up_with_casts(k, hq, ss):
        w2s, layer = ffns[k][2], ffns[k][3]
        jobs = [_CastJob(w2s, layer, f_hidden, ((0, d, d),))]
        if k + 1 < len(ffns):
            n1, n3, _, nl = ffns[k + 1]
            jobs += [up_job(n1, nl), up_job(n3, nl)]
        mix_layer = 0 if k == 0 else (k + 1) // 2 if k % 2 == 1 and (k + 1) // 2 < depth else None
        if mix_layer is not None:
            jobs += [_CastJob(w_in, mix_layer, d, in_windows),
                     _CastJob(w_out, mix_layer, w_out.shape[1], ((0, d, d),))]
        hmid, casts = _ffn_up(hq, ss, up_w[0], up_w[1], 0, f_hidden, 0.5, tuple(jobs))
        if k + 1 < len(ffns):
            up_w[:] = [casts[1][0], casts[2][0]]
        if mix_layer is not None:
            mix_w[mix_layer] = casts[-2] + casts[-1]
        return hmid, casts[0][0]

    ffn_rows = (512, 256, 128, 64)
    out_rows = (1024, 512, 256, 128, 64)

    outs = {k: [] for k in ("conv_p", "gdn_p", "ret_p", "pool_p", "conv_s", "gdn_s", "ret_s", "pool_s")}
    for l in range(depth):
        hmid, w2 = ffn_up_with_casts(2 * l, hq, ss)
        x, hq, ss = _proj_residual(hmid, w2, 0, x, mix_norm[l], ffn_rows)
        w_a, w_ab, w_b, w_c, w_o = mix_w[l]
        pa, pab, pb, pc = (_in_proj(hq, ss, w, 0) for w in (w_a, w_ab, w_b, w_c))
        mix, sa, ca, sb, pbuf = _mixers(pa, pab, pb, pc, state_conv, state_gdn, state_ret, state_pool, l, conv_w[l],
                                        gdn_a_log[l], gdn_dt_bias[l], gdn_norm_w[l], pool_w[l], pool_scale[l],
                                        sched, n_pos)
        x, hq, ss = _proj_residual(mix, w_o, 0, x, ffn2_norm[l], out_rows)
        hmid, w2 = ffn_up_with_casts(2 * l + 1, hq, ss)
        if l + 1 < depth:
            x, hq, ss = _proj_residual(hmid, w2, 0, x, ffn1_norm[l + 1], ffn_rows)
        else:
            x = _proj_residual(hmid, w2, 0, x, None, ffn_rows)
        for k, v in (("conv", ca), ("pool", pbuf), ("gdn", sa), ("ret", sb)):
            outs[k + "_p"].append(v[:bp])
            outs[k + "_s"].append(v[bp:])

    y_prompt = _final_norm(x, final_norm, 0, mp).reshape(bp, lp, d)
    y_sample = _final_norm(x, final_norm, mp, ms).reshape(bs, ls, d)
    st = {k: jnp.stack(v) for k, v in outs.items()}
    return (y_prompt, y_sample, st["conv_p"], st["gdn_p"], st["ret_p"], st["pool_p"], st["conv_s"], st["gdn_s"],
            st["ret_s"], st["pool_s"])
```

```python
import functools
import math
from typing import NamedTuple

import jax
import jax.numpy as jnp
from jax import lax
from jax.experimental import pallas as pl
from jax.experimental.pallas import tpu as pltpu

CHUNK = 64
EPS = 1e-6
ROPE_BASE = 10000.0
POOL_WINDOWS = (2, 4, 8, 16)
PAST_LEN = 4096
LANES = 128
GDN_HEAD_GROUP = 12
FFN_TILE = 512
VMEM_LIMIT_BYTES = 56 * 1024 * 1024

f32 = jnp.float32
bf16 = jnp.bfloat16
HI = lax.Precision.HIGHEST


def _pick(n, candidates):
    for c in candidates:
        if n % c == 0:
            return c
    raise ValueError(f"no block size in {candidates} divides {n}")


def _params(sem):
    return pltpu.CompilerParams(dimension_semantics=sem, vmem_limit_bytes=VMEM_LIMIT_BYTES)


def _silu(x):
    return x * jax.nn.sigmoid(x)


def _dot(a, b):
    return jnp.dot(a, b, preferred_element_type=f32)


def _dot_hi(a, b):
    return jnp.dot(a, b, preferred_element_type=f32, precision=HI)


def _dot_nt(a, b):
    return lax.dot_general(a, b, (((1,), (1,)), ((), ())), preferred_element_type=f32)


def _dot_tn(a, b):
    return lax.dot_general(a, b, (((0,), (0,)), ((), ())), preferred_element_type=f32)


def _row_rsqrt(ss_ref, d_model):
    return lax.rsqrt(ss_ref[:, 0:1] / d_model + EPS)


def _norm_prep_kernel(*refs, na, job_dims):
    n_jobs = len(job_dims)
    xa_ref, xb_ref, g_ref = refs[:3]
    src_refs = refs[3:3 + n_jobs]
    x_ref, hq_ref, ss_ref = refs[3 + n_jobs:6 + n_jobs]
    dst_refs = refs[6 + n_jobs:]

    def emit(src_ref):
        x = src_ref[...]
        x_ref[...] = x
        hq_ref[...] = (x * g_ref[...]).astype(hq_ref.dtype)
        ss_ref[...] = jnp.broadcast_to(jnp.sum(x * x, axis=-1, keepdims=True), ss_ref.shape)

    i = pl.program_id(0)
    pl.when(i < na)(lambda: emit(xa_ref))
    pl.when(i >= na)(lambda: emit(xb_ref))
    _run_jobs(job_dims, src_refs, dst_refs, i)


def _norm_prep(xa, xb, g, jobs=()):
    (ma, d), mb = xa.shape, xb.shape[0]
    bm = _pick(math.gcd(ma, mb), (256, 128, 64))
    na, m = ma // bm, ma + mb
    job_in, job_out, job_shape, job_dims = _job_plan(jobs, m // bm, lambda i: i)
    row = pl.BlockSpec((bm, d), lambda i: (i, 0))
    res = pl.pallas_call(
        functools.partial(_norm_prep_kernel, na=na, job_dims=job_dims),
        grid=(m // bm,),
        in_specs=[
            pl.BlockSpec((bm, d), lambda i: (jnp.minimum(i, na - 1), 0)),
            pl.BlockSpec((bm, d), lambda i: (jnp.maximum(i - na, 0), 0)),
            pl.BlockSpec((1, d), lambda i: (0, 0)),
        ] + job_in,
        out_specs=[row, row, pl.BlockSpec((bm, LANES), lambda i: (i, 0))] + job_out,
        out_shape=[jax.ShapeDtypeStruct((m, d), f32), jax.ShapeDtypeStruct((m, d), bf16),
                   jax.ShapeDtypeStruct((m, LANES), f32)] + job_shape,
        compiler_params=_params(("arbitrary",)),
        name="norm_prep",
    )(xa, xb, g.reshape(1, d), *[job.src for job in jobs])
    return res[0], res[1], res[2], _split_copies(jobs, res[3:])


def _final_norm_kernel(x_ref, g_ref, o_ref):
    x = x_ref[...]
    o_ref[...] = x * lax.rsqrt(jnp.mean(x * x, axis=-1, keepdims=True) + EPS) * g_ref[...]


def _final_norm(x, g, row0, n_rows):
    d = x.shape[1]
    bm = _pick(n_rows, (512, 256, 128, 64))
    assert row0 % bm == 0
    b0 = row0 // bm
    return pl.pallas_call(
        _final_norm_kernel,
        grid=(n_rows // bm,),
        in_specs=[pl.BlockSpec((bm, d), lambda i: (i + b0, 0)), pl.BlockSpec((1, d), lambda i: (0, 0))],
        out_specs=pl.BlockSpec((bm, d), lambda i: (i, 0)),
        out_shape=jax.ShapeDtypeStruct((n_rows, d), f32),
        compiler_params=_params(("parallel",)),
        name="final_norm",
    )(x, g.reshape(1, d))


class _CastJob(NamedTuple):
    src: jax.Array
    layer: int
    rows_out: int
    windows: tuple


def _job_tile_rows(job, n_steps):
    rows = job.src.shape[1]
    for tr in (16, 32, 64, 128, 256, 512, 1024):
        if rows % tr == 0 and job.rows_out % tr == 0 and job.rows_out // tr <= n_steps:
            return tr
    raise ValueError("cast job does not fit the host grid")


def _job_plan(jobs, n_steps, step_of):
    in_specs, out_specs, out_shape, dims = [], [], [], []
    for job in jobs:
        tr = _job_tile_rows(job, n_steps)
        n_src, n_all = job.src.shape[1] // tr, job.rows_out // tr
        in_specs.append(pl.BlockSpec(
            (None, tr, job.src.shape[2]),
            lambda *g, n_src=n_src, layer=job.layer: (layer, jnp.minimum(step_of(*g), n_src - 1), 0)))
        for _, _, cols_out in job.windows:
            out_specs.append(pl.BlockSpec(
                (tr, cols_out), lambda *g, n_all=n_all: (jnp.minimum(step_of(*g), n_all - 1), 0)))
            out_shape.append(jax.ShapeDtypeStruct((job.rows_out, cols_out), bf16))
        dims.append((tr, job.src.shape[1], job.rows_out, job.windows))
    return in_specs, out_specs, out_shape, tuple(dims)


def _run_jobs(job_dims, src_refs, dst_refs, t):
    dst_refs = list(dst_refs)
    for (tr, rows, rows_out, windows), src_ref in zip(job_dims, src_refs):
        for col0, cols, cols_out in windows:
            dst_ref = dst_refs.pop(0)
            val = src_ref[:, col0:col0 + cols].astype(bf16)
            if rows_out > rows:
                val = jnp.where(t < rows // tr, val, jnp.zeros_like(val))
            dst_ref[:, 0:cols] = val
            if cols_out > cols:
                dst_ref[:, cols:cols_out] = jnp.zeros((tr, cols_out - cols), bf16)


def _split_copies(jobs, outs):
    copies, k = [], 0
    for job in jobs:
        copies.append(list(outs[k:k + len(job.windows)]))
        k += len(job.windows)
    return copies


def _up_kernel(*refs, d_model, scale, job_dims, nj, last_cols):
    n_jobs = len(job_dims)
    h_ref, ss_ref, w1_ref, w3_ref = refs[:4]
    src_refs, o_ref, dst_refs = refs[4:4 + n_jobs], refs[4 + n_jobs], refs[5 + n_jobs:]
    bm, bn = o_ref.shape

    def tile(cols):
        h = h_ref[...]
        r = _row_rsqrt(ss_ref, d_model)
        g = _dot(h, w1_ref[:, 0:cols]) * r
        u = _dot(h, w3_ref[:, 0:cols]) * r
        o_ref[:, 0:cols] = (scale * _silu(g) * u).astype(o_ref.dtype)
        if cols < bn:
            o_ref[:, cols:bn] = jnp.zeros((bm, bn - cols), o_ref.dtype)

    if last_cols == bn:
        tile(bn)
    else:
        j = pl.program_id(1)
        pl.when(j != nj // 2)(lambda: tile(bn))
        pl.when(j == nj // 2)(lambda: tile(last_cols))
    _run_jobs(job_dims, src_refs, dst_refs, pl.program_id(0) * nj + pl.program_id(1))


def _short_tile_mid(j, nj):
    return jnp.where(j < nj // 2, j, jnp.where(j == nj // 2, nj - 1, j - 1))


def _wspec(w, layer, row0, rows, bn, col_tile=lambda j: j):
    assert row0 % rows == 0
    if w.ndim == 2:
        return pl.BlockSpec((rows, bn), lambda i, j: (row0 // rows, col_tile(j)))
    return pl.BlockSpec((None, rows, bn), lambda i, j: (layer, row0 // rows, col_tile(j)))


def _ffn_up(hq, ss, w1, w3, layer, f_valid, scale, jobs=()):
    m, d = hq.shape
    f = w1.shape[-1]
    bm = _pick(m, (1024, 512, 256, 128, 64))
    bn = _pick(f, (FFN_TILE, 256, 128))
    ni, nj = m // bm, f // bn
    last_cols = f_valid - (nj - 1) * bn
    assert 0 < last_cols <= bn and last_cols % LANES == 0
    job_in, job_out, job_shape, job_dims = _job_plan(jobs, ni * nj, lambda i, j: i * nj + j)
    col_tile = (lambda j: _short_tile_mid(j, nj)) if last_cols < bn else (lambda j: j)
    in_specs = [
        pl.BlockSpec((bm, d), lambda i, j: (i, 0)),
        pl.BlockSpec((bm, LANES), lambda i, j: (i, 0)),
        _wspec(w1, layer, 0, d, bn, col_tile),
        _wspec(w3, layer, 0, d, bn, col_tile),
    ]
    res = pl.pallas_call(
        functools.partial(_up_kernel, d_model=d, scale=scale, job_dims=job_dims, nj=nj, last_cols=last_cols),
        grid=(ni, nj),
        in_specs=in_specs + job_in,
        out_specs=[pl.BlockSpec((bm, bn), lambda i, j: (i, col_tile(j)))] + job_out,
        out_shape=[jax.ShapeDtypeStruct((m, f), bf16)] + job_shape,
        compiler_params=_params(("arbitrary", "arbitrary")),
        name="ffn_up",
    )(hq, ss, w1, w3, *[job.src for job in jobs])
    return res[0], _split_copies(jobs, res[1:])


def _in_proj_kernel(h_ref, ss_ref, w_ref, o_ref, *, d_model):
    o_ref[...] = _dot(h_ref[...], w_ref[...]) * _row_rsqrt(ss_ref, d_model)


def _in_proj(hq, ss, w, layer):
    m, d = hq.shape
    n = w.shape[-1]
    bm = _pick(m, (1024, 512, 256, 128, 64))
    bn = _pick(n, (1024, 768, 512, 384, 256, 128))
    return pl.pallas_call(
        functools.partial(_in_proj_kernel, d_model=d),
        grid=(m // bm, n // bn),
        in_specs=[
            pl.BlockSpec((bm, d), lambda i, j: (i, 0)),
            pl.BlockSpec((bm, LANES), lambda i, j: (i, 0)),
            _wspec(w, layer, 0, d, bn),
        ],
        out_specs=pl.BlockSpec((bm, bn), lambda i, j: (i, j)),
        out_shape=jax.ShapeDtypeStruct((m, n), f32),
        compiler_params=_params(("parallel", "arbitrary")),
        name="in_proj",
    )(hq, ss, w)


def _proj_residual_kernel(a_ref, w_ref, x_ref, *rest, with_norm):
    xn = x_ref[...] + _dot(a_ref[:, 0:w_ref.shape[0]], w_ref[...])
    if not with_norm:
        rest[0][...] = xn
        return
    g_ref, xo_ref, hq_ref, ss_ref = rest
    xo_ref[...] = xn
    hq_ref[...] = (xn * g_ref[...]).astype(hq_ref.dtype)
    part = jnp.broadcast_to(jnp.sum(xn * xn, axis=-1, keepdims=True), ss_ref.shape)
    j = pl.program_id(1)

    @pl.when(j == 0)
    def _():
        ss_ref[...] = part

    @pl.when(j != 0)
    def _():
        ss_ref[...] += part


def _proj_residual(a, w, layer, x, g, bm_cands):
    m, d = x.shape
    bm = _pick(m, bm_cands)
    bn = _pick(d, (512, 256, 128))
    assert w.shape[-1] == d and w.shape[-2] <= a.shape[1] and w.shape[-2] % LANES == 0
    with_norm = g is not None
    tile = pl.BlockSpec((bm, bn), lambda i, j: (i, j))
    in_specs = [pl.BlockSpec((bm, a.shape[1]), lambda i, j: (i, 0)), _wspec(w, layer, 0, w.shape[-2], bn), tile]
    args = [a, w, x]
    if with_norm:
        in_specs.append(pl.BlockSpec((1, bn), lambda i, j: (0, j)))
        args.append(g.reshape(1, d))
        out_specs = [tile, tile, pl.BlockSpec((bm, LANES), lambda i, j: (i, 0))]
        out_shape = [jax.ShapeDtypeStruct((m, d), f32), jax.ShapeDtypeStruct((m, d), bf16),
                     jax.ShapeDtypeStruct((m, LANES), f32)]
    else:
        out_specs, out_shape = tile, jax.ShapeDtypeStruct((m, d), f32)
    return pl.pallas_call(
        functools.partial(_proj_residual_kernel, with_norm=with_norm),
        grid=(m // bm, d // bn),
        in_specs=in_specs,
        out_specs=out_specs,
        out_shape=out_shape,
        compiler_params=_params(("parallel", "arbitrary")),
        name="proj_residual",
    )(*args)


class _Sched:
    def __init__(self, bp, lp, bs, ls):
        assert lp % CHUNK == 0 and ls % CHUNK == 0 and PAST_LEN % CHUNK == 0
        self.bp, self.np_, self.bs, self.ns = bp, lp // CHUNK, bs, ls // CHUNK
        self.n_prompt = self.bp * self.np_
        self.n_chunks = self.n_prompt + self.bs * self.ns
        self.n_seq = bp + bs

    def split(self, c):
        in_prompt = c < self.n_prompt
        cs = jnp.maximum(c - self.n_prompt, 0)
        seq = jnp.where(in_prompt, c // self.np_, self.bp + cs // self.ns)
        ci = jnp.where(in_prompt, c % self.np_, cs % self.ns)
        last = jnp.where(in_prompt, self.np_ - 1, self.ns - 1)
        return in_prompt, seq, ci, ci == last

    def seq(self, c):
        return self.split(c)[1]

    def sample_seq(self, c):
        return jnp.maximum(self.seq(c) - self.bp, 0)

    def pos_chunk(self, c):
        in_prompt, _, ci, _ = self.split(c)
        return jnp.where(in_prompt, 0, PAST_LEN // CHUNK) + ci


def _strict_lower_inverse_minus_eye(lows, row, col):
    pair = (row // 2) == (col // 2)
    es = [-jnp.where(pair, low, 0.0) for low in lows]
    b = 2
    while b < CHUNK:
        mask = ((row // (2 * b)) == (col // (2 * b))) & ((row // b) != (col // b))
        cs = [jnp.where(mask, low, 0.0) for low in lows]
        xs = [c + _dot(e.astype(bf16), c.astype(bf16)) for e, c in zip(es, cs)]
        ys = [x + _dot(x.astype(bf16), e.astype(bf16)) for x, e in zip(xs, es)]
        es = [e - y for e, y in zip(es, ys)]
        b *= 2
    return es


def _gdn_kernel(qkv_ref, z_ref, ab_ref, cinit_ref, sinit_ref, cw_ref, alog_ref, dtb_ref, nw_ref,
                o_ref, sout_ref, cout_ref, hist_ref, s_ref, gt_ref, *, sched, n_heads, dk, dv, phase):
    c = pl.program_id(0)
    in_prompt, _, ci, is_last = sched.split(c)
    w = n_heads * dk
    hist = hist_ref.shape[0]
    kw = cw_ref.shape[0]

    if phase == "init":
        @pl.when((ci == 0) & in_prompt)
        def _():
            hist_ref[...] = jnp.zeros(hist_ref.shape, f32)
            s_ref[...] = jnp.zeros(s_ref.shape, f32)

        @pl.when((ci == 0) & jnp.logical_not(in_prompt))
        def _():
            hist_ref[0:hist - (kw - 1), :] = jnp.zeros((hist - (kw - 1), hist_ref.shape[1]), f32)
            hist_ref[hist - (kw - 1):hist, :] = cinit_ref[0]
            s_ref[...] = sinit_ref[0]
        return

    if phase == "finish":
        @pl.when(is_last)
        def _():
            sout_ref[0] = s_ref[...]
            cout_ref[0] = qkv_ref[CHUNK - (kw - 1):CHUNK, :]
        return

    ab = ab_ref[...]
    sp = jnp.maximum(ab + dtb_ref[...], 0.0) + jnp.log1p(jnp.exp(-jnp.abs(ab + dtb_ref[...])))
    g_all = -jnp.exp(alog_ref[...]) * sp
    beta_all = jax.nn.sigmoid(ab)

    r2 = lax.broadcasted_iota(jnp.int32, (2 * CHUNK, CHUNK), 0)
    c2 = lax.broadcasted_iota(jnp.int32, (2 * CHUNK, CHUNK), 1)
    gp = _dot_hi((c2 <= r2).astype(f32), g_all)
    gt_ref[...] = gp.T
    g_cum = gp[0:CHUNK, :]
    eg_all = jnp.exp(g_cum)
    g_last = g_cum[CHUNK - 1:CHUNK, :]
    tail_all = jnp.exp(g_last - g_cum)
    gblk_all = jnp.exp(g_last)

    row = lax.broadcasted_iota(jnp.int32, (CHUNK, CHUNK), 0)
    col = lax.broadcasted_iota(jnp.int32, (CHUNK, CHUNK), 1)
    causal = row >= col
    strict = row > col

    def conv(off):
        xt = jnp.concatenate([hist_ref[:, off:off + dk], qkv_ref[:, off:off + dk]], axis=0)
        acc = xt * cw_ref[0:1, off:off + dk]
        for i in range(1, kw):
            acc = pltpu.roll(acc, 1, axis=0) + xt * cw_ref[i:i + 1, off:off + dk]
        return _silu(acc[hist:hist + CHUNK, :])

    def l2n(t, scale=1.0):
        return t * (lax.rsqrt(jnp.sum(t * t, axis=-1, keepdims=True) + EPS) * scale)

    for g0 in range(0, n_heads, GDN_HEAD_GROUP):
        heads = range(g0, min(g0 + GDN_HEAD_GROUP, n_heads))
        lows, rhs_k, ktail_b, qdec_b, qkd_b, bc = [], {}, {}, {}, {}, {}
        for h in heads:
            bc[h] = beta_all[:, n_heads + h:n_heads + h + 1]
            egc = eg_all[:, h:h + 1]
            kn = l2n(conv(w + h * dk))
            kb = kn.astype(bf16)
            kk = _dot_nt(kb, kb)
            qn = l2n(conv(h * dk), dk ** -0.5)
            qk = _dot_nt(qn.astype(bf16), kb)
            decay = jnp.where(causal, jnp.exp(jnp.where(
                causal, g_cum[:, h:h + 1] - gt_ref[h:h + 1, 0:CHUNK], 0.0)), 0.0)
            lows.append(jnp.where(strict, kk * decay * bc[h], 0.0))
            qkd_b[h] = (qk * decay).astype(bf16)
            qdec_b[h] = (qn * egc).astype(bf16)
            ktail_b[h] = (kn * tail_all[:, h:h + 1]).astype(bf16)
            rhs_k[h] = kn * (bc[h] * egc)
        es = dict(zip(heads, _strict_lower_inverse_minus_eye(lows, row, col)))
        rhs = {h: jnp.concatenate([conv(2 * w + h * dv) * bc[h], rhs_k[h]], axis=-1) for h in heads}
        sol = {h: rhs[h] + _dot(es[h].astype(bf16), rhs[h].astype(bf16)) for h in heads}

        sb = {h: s_ref[h].astype(bf16) for h in heads}
        ws = {h: _dot(sol[h][:, dv:dv + dk].astype(bf16), sb[h]) for h in heads}
        ub = {h: (sol[h][:, 0:dv] - ws[h]).astype(bf16) for h in heads}
        qs = {h: _dot(qdec_b[h], sb[h]) for h in heads}
        qu = {h: _dot(qkd_b[h], ub[h]) for h in heads}
        ku = {h: _dot_tn(ktail_b[h], ub[h]) for h in heads}
        for h in heads:
            s_ref[h] = s_ref[h] * gblk_all[:, h:h + 1] + ku[h]
            o = qs[h] + qu[h]
            o = o * lax.rsqrt(jnp.mean(o * o, axis=-1, keepdims=True) + EPS) * nw_ref[...]
            o = o * _silu(z_ref[:, h * dv:(h + 1) * dv])
            o_ref[:, h * dv:(h + 1) * dv] = o.astype(o_ref.dtype)
    hist_ref[...] = qkv_ref[CHUNK - hist:CHUNK, :]


N_GDN_IN, N_RETPOOL_IN = 9, 15


def _mixer_kernel(*refs, sched, gdn_cfg, ret_cfg, gc):
    gdn_in = refs[:N_GDN_IN]
    rp_in = refs[N_GDN_IN:N_GDN_IN + N_RETPOOL_IN]
    mix_ref, sa_ref, cout_ref, sb_ref, bout_ref, xp_ref, s_a, gt_ref, s_b, up_ref = refs[N_GDN_IN + N_RETPOOL_IN:]
    wa = gdn_cfg["n_heads"] * gdn_cfg["dv"]
    wb = ret_cfg["n_heads"] * ret_cfg["dv"]
    wc = mix_ref.shape[1] - wa - wb
    for phase in ("init", "body", "finish"):
        _gdn_kernel(*gdn_in, mix_ref.at[:, pl.ds(0, wa)], sa_ref, cout_ref, xp_ref, s_a, gt_ref, sched=sched,
                    phase=phase, **gdn_cfg)
        _retpool_kernel(*rp_in, mix_ref.at[:, pl.ds(wa, wb)], sb_ref, mix_ref.at[:, pl.ds(wa + wb, wc)], bout_ref,
                        s_b, up_ref, sched=sched, gc=gc, phase=phase, **ret_cfg)


def _mixers(pa, pab, pb, pc, conv_init, gdn_init, ret_init, buf_init, layer, conv_w, a_log, dt_bias, norm_w,
            pool_w, pool_scale, sched, n_pos):
    m, width = pc.shape
    _, _, ha, dka, dva = gdn_init.shape
    _, _, hb, dkb, dvb = ret_init.shape
    n_seq = sched.n_seq
    assert gdn_init.shape[1] == sched.bs
    n_grp, gc, _ = pool_w.shape
    nbuf = buf_init.shape[2]
    kw = conv_w.shape[0]
    assert dka == dva == LANES and 2 * ha <= LANES
    assert dkb == LANES and dvb == 2 * dkb
    assert n_grp == len(POOL_WINDOWS) and nbuf == max(POOL_WINDOWS) - 1 and gc % LANES == 0
    wa, qw, vw = ha * dka, hb * dkb, hb * dvb
    pad = lambda t: jnp.pad(t.reshape(1, -1), ((0, 0), (0, LANES - t.size)))
    cos2, sin2, dmat, qdec, ktail, gblk = _ret_tables(hb, dkb, dvb, n_pos)
    whole = lambda a: pl.BlockSpec(a.shape, lambda c: (0,) * a.ndim)
    seq = sched.seq
    gdn_specs = [
        pl.BlockSpec((CHUNK, 3 * wa), lambda c: (c, 0)),
        pl.BlockSpec((CHUNK, wa), lambda c: (c, 3)),
        pl.BlockSpec((CHUNK, LANES), lambda c: (c, 0)),
        pl.BlockSpec((None, 1, kw - 1, 3 * wa), lambda c: (layer, sched.sample_seq(c), 0, 0)),
        pl.BlockSpec((None, 1, ha, dka, dva), lambda c: (layer, sched.sample_seq(c), 0, 0, 0)),
        pl.BlockSpec((kw, 3 * wa), lambda c: (0, 0)),
        pl.BlockSpec((1, LANES), lambda c: (0, 0)),
        pl.BlockSpec((1, LANES), lambda c: (0, 0)),
        pl.BlockSpec((1, dva), lambda c: (0, 0)),
    ]
    gdn_args = [pa, pa, pab, conv_init, gdn_init, conv_w, pad(a_log), pad(dt_bias), norm_w.reshape(1, dva)]
    rp_specs = [
        pl.BlockSpec((CHUNK, qw), lambda c: (c, 0)),
        pl.BlockSpec((CHUNK, qw), lambda c: (c, 1)),
        pl.BlockSpec((CHUNK, vw), lambda c: (c, 1)),
        pl.BlockSpec((CHUNK, vw), lambda c: (c, 2)),
        pl.BlockSpec((CHUNK, dkb), lambda c: (sched.pos_chunk(c), 0)),
        pl.BlockSpec((CHUNK, dkb), lambda c: (sched.pos_chunk(c), 0)),
        whole(dmat), whole(qdec), whole(ktail), whole(gblk),
        pl.BlockSpec((None, 1, hb, dkb, dvb), lambda c: (layer, sched.sample_seq(c), 0, 0, 0)),
        pl.BlockSpec((CHUNK, width), lambda c: (c, 0)),
        pl.BlockSpec((None, 1, nbuf, width), lambda c: (layer, sched.sample_seq(c), 0, 0)),
        pl.BlockSpec((n_grp, gc, gc), lambda c: (0, 0, 0)),
        pl.BlockSpec((1, width), lambda c: (0, 0)),
    ]
    rp_args = [pb, pb, pb, pb, cos2, sin2, dmat, qdec, ktail, gblk, ret_init, pc, buf_init, pool_w.astype(bf16),
               pool_scale.reshape(1, width)]
    assert len(gdn_specs) == N_GDN_IN and len(rp_specs) == N_RETPOOL_IN
    d_mix = wa + vw + width
    return pl.pallas_call(
        functools.partial(_mixer_kernel, sched=sched, gdn_cfg=dict(n_heads=ha, dk=dka, dv=dva),
                          ret_cfg=dict(n_heads=hb, dk=dkb, dv=dvb), gc=gc),
        grid=(sched.n_chunks,),
        in_specs=gdn_specs + rp_specs,
        out_specs=[
            pl.BlockSpec((CHUNK, d_mix), lambda c: (c, 0)),
            pl.BlockSpec((1, ha, dka, dva), lambda c: (seq(c), 0, 0, 0)),
            pl.BlockSpec((1, kw - 1, 3 * wa), lambda c: (seq(c), 0, 0)),
            pl.BlockSpec((1, hb, dkb, dvb), lambda c: (seq(c), 0, 0, 0)),
            pl.BlockSpec((1, nbuf, width), lambda c: (seq(c), 0, 0)),
        ],
        out_shape=[
            jax.ShapeDtypeStruct((m, d_mix), bf16),
            jax.ShapeDtypeStruct((n_seq, ha, dka, dva), f32),
            jax.ShapeDtypeStruct((n_seq, kw - 1, 3 * wa), f32),
            jax.ShapeDtypeStruct((n_seq, hb, dkb, dvb), f32),
            jax.ShapeDtypeStruct((n_seq, nbuf, width), f32),
        ],
        scratch_shapes=[
            pltpu.VMEM((8, 3 * wa), f32),
            pltpu.VMEM((ha, dka, dva), f32),
            pltpu.VMEM((LANES, 2 * CHUNK), f32),
            pltpu.VMEM((hb, dkb, dvb), f32),
            pltpu.VMEM((16, width), f32),
        ],
        compiler_params=_params(("arbitrary",)),
        name="mixers",
    )(*gdn_args, *rp_args)


def _retpool_kernel(q_ref, k_ref, v_ref, g_ref, cos_ref, sin_ref, dmat_ref, qdec_ref, ktail_ref, gblk_ref, sinit_ref,
                    u_ref, binit_ref, pw_ref, pscale_ref, o_ref, sout_ref, oc_ref, bout_ref, s_ref, hist_ref,
                    *, sched, n_heads, dk, dv, gc, phase):
    c = pl.program_id(0)
    in_prompt, _, ci, is_last = sched.split(c)
    hist = hist_ref.shape[0]
    nbuf = binit_ref.shape[1]

    if phase == "init":
        @pl.when((ci == 0) & in_prompt)
        def _():
            s_ref[...] = jnp.zeros(s_ref.shape, f32)
            hist_ref[...] = jnp.zeros(hist_ref.shape, f32)

        @pl.when((ci == 0) & jnp.logical_not(in_prompt))
        def _():
            s_ref[...] = sinit_ref[0]
            hist_ref[0:hist - nbuf, :] = jnp.zeros((hist - nbuf, hist_ref.shape[1]), f32)
            hist_ref[hist - nbuf:hist, :] = binit_ref[0]
        return

    if phase == "finish":
        @pl.when(is_last)
        def _():
            sout_ref[0] = s_ref[...]
            bout_ref[0] = u_ref[CHUNK - nbuf:CHUNK, :]
        return

    cos = cos_ref[...]
    sin = sin_ref[...]

    def rope(t):
        return t * cos + pltpu.roll(t, dk // 2, axis=1) * sin

    heads = range(n_heads)
    qr = [rope(q_ref[:, h * dk:(h + 1) * dk]) for h in heads]
    kr = [rope(k_ref[:, h * dk:(h + 1) * dk]) * (dk ** -0.5) for h in heads]
    vb = [v_ref[:, h * dv:(h + 1) * dv].astype(bf16) for h in heads]
    scores = [_dot_nt(qr[h].astype(bf16), kr[h].astype(bf16)) * dmat_ref[h] for h in heads]
    cross = [_dot((qr[h] * qdec_ref[h]).astype(bf16), s_ref[h].astype(bf16)) for h in heads]
    inner = [_dot(scores[h].astype(bf16), vb[h]) for h in heads]
    kv = [_dot_tn((kr[h] * ktail_ref[h]).astype(bf16), vb[h]) for h in heads]
    for h in heads:
        s_ref[h] = s_ref[h] * gblk_ref[h] + kv[h]
        o = inner[h] + cross[h]
        oc = o - jnp.mean(o, axis=-1, keepdims=True)
        on = oc * lax.rsqrt(jnp.mean(oc * oc, axis=-1, keepdims=True) + EPS)
        o_ref[:, h * dv:(h + 1) * dv] = (on * _silu(g_ref[:, h * dv:(h + 1) * dv])).astype(o_ref.dtype)

    pos = (jnp.where(in_prompt, 0, PAST_LEN) + ci * CHUNK
           + lax.broadcasted_iota(jnp.int32, (CHUNK, 1), 0))
    for gi, win_len in enumerate(POOL_WINDOWS):
        cols = slice(gi * gc, (gi + 1) * gc)
        ext = jnp.concatenate([hist_ref[:, cols], u_ref[:, cols]], axis=0)
        acc, span = ext, 1
        while span < win_len:
            acc = acc + pltpu.roll(acc, span, axis=0)
            span *= 2
        assert span == win_len and win_len - 1 <= hist
        cur = ext[hist:hist + CHUNK, :]
        cnt = jnp.minimum(pos + 1, win_len).astype(f32)
        d = acc[hist:hist + CHUNK, :] / cnt - cur
        y = _dot(d.astype(bf16), pw_ref[gi]) * pscale_ref[:, cols]
        oc_ref[:, cols] = y.astype(oc_ref.dtype)
    hist_ref[...] = u_ref[CHUNK - hist:CHUNK, :]


def _ret_tables(n_heads, dk, dv, n_pos):
    half = dk // 2
    inv_freq = ROPE_BASE ** (-jnp.arange(half, dtype=f32) / half)
    ang = jnp.arange(n_pos, dtype=jnp.int32).astype(f32)[:, None] * inv_freq[None, :]
    cos, sin = jnp.cos(ang), jnp.sin(ang)
    cos2 = jnp.concatenate([cos, cos], axis=-1)
    sin2 = jnp.concatenate([-sin, sin], axis=-1)
    lg = jnp.log1p(-jnp.exp2(-5.0 - jnp.arange(n_heads, dtype=f32)))
    idx = jnp.arange(CHUNK, dtype=f32)
    rel = idx[:, None] - idx[None, :]
    causal = rel >= 0
    dmat = jnp.where(causal, jnp.exp(jnp.where(causal, rel, 0.0) * lg[:, None, None]), 0.0)
    qdec = jnp.exp((idx + 1.0)[None, :] * lg[:, None])
    ktail = jnp.exp((CHUNK - 1.0 - idx)[None, :] * lg[:, None])
    gblk = jnp.exp(CHUNK * lg)
    bl = lambda t: jnp.broadcast_to(t[:, :, None], (n_heads, CHUNK, dk))
    return cos2, sin2, dmat, bl(qdec), bl(ktail), jnp.broadcast_to(gblk[:, None, None], (n_heads, 1, dv))


def kernel(x_prompt, x_sample, state_conv, state_gdn, state_ret, state_pool, ffn1_norm, ffn1_w1, ffn1_w3, ffn1_w2,
           mix_norm, w_in, conv_w, gdn_a_log, gdn_dt_bias, gdn_norm_w, pool_w, pool_scale, w_out, ffn2_norm,
           ffn2_w1, ffn2_w3, ffn2_w2, final_norm):
    bp, lp, d = x_prompt.shape
    bs, ls, _ = x_sample.shape
    depth = ffn1_w1.shape[0]
    sched = _Sched(bp, lp, bs, ls)
    mp, ms = bp * lp, bs * ls
    n_pos = max(lp, PAST_LEN + ls)

    _, _, ha, dka, dva = state_gdn.shape
    _, _, hb, dkb, dvb = state_ret.shape
    wa, qkb, wb = ha * dva, hb * dkb, hb * dvb
    pool_width = state_pool.shape[-1]
    in_sizes = (2 * ha * dka + wa, wa, ha, ha, qkb, qkb, wb, wb, pool_width)
    assert sum(in_sizes) == w_in.shape[-1]
    offs = [0]
    for s in in_sizes:
        offs.append(offs[-1] + s)

    f_hidden = ffn1_w1.shape[-1]
    f_pad = f_hidden + (-f_hidden % FFN_TILE)
    ffns = []
    for l in range(depth):
        ffns += [(ffn1_w1, ffn1_w3, ffn1_w2, l), (ffn2_w1, ffn2_w3, ffn2_w2, l)]
    up_job = lambda w, layer: _CastJob(w, layer, d, ((0, f_hidden, f_pad),))
    x, hq, ss, casts = _norm_prep(x_prompt.reshape(mp, d), x_sample.reshape(ms, d), ffn1_norm[0],
                                  (up_job(ffn1_w1, 0), up_job(ffn1_w3, 0)))
    up_w = [casts[0][0], casts[1][0]]
    span = lambda i, j, out=None: (offs[i], offs[j] - offs[i], out or offs[j] - offs[i])
    in_windows = (span(0, 2), span(2, 4, LANES), span(4, 8), span(8, 9))
    mix_w = {}

    def ffn_up_with_casts(k, hq, ss):
        w2s, layer = ffns[k][2], ffns[k][3]
        jobs = [_CastJob(w2s, layer, f_hidden, ((0, d, d),))]
        if k + 1 < len(ffns):
            n1, n3, _, nl = ffns[k + 1]
            jobs += [up_job(n1, nl), up_job(n3, nl)]
        mix_layer = 0 if k == 0 else (k + 1) // 2 if k % 2 == 1 and (k + 1) // 2 < depth else None
        if mix_layer is not None:
            jobs += [_CastJob(w_in, mix_layer, d, in_windows),
                     _CastJob(w_out, mix_layer, w_out.shape[1], ((0, d, d),))]
        hmid, casts = _ffn_up(hq, ss, up_w[0], up_w[1], 0, f_hidden, 0.5, tuple(jobs))
        if k + 1 < len(ffns):
            up_w[:] = [casts[1][0], casts[2][0]]
        if mix_layer is not None:
            mix_w[mix_layer] = casts[-2] + casts[-1]
        return hmid, casts[0][0]

    ffn_rows = (512, 256, 128, 64)
    out_rows = (1024, 512, 256, 128, 64)

    outs = {k: [] for k in ("conv_p", "gdn_p", "ret_p", "pool_p", "conv_s", "gdn_s", "ret_s", "pool_s")}
    for l in range(depth):
        hmid, w2 = ffn_up_with_casts(2 * l, hq, ss)
        x, hq, ss = _proj_residual(hmid, w2, 0, x, mix_norm[l], ffn_rows)
        w_a, w_ab, w_b, w_c, w_o = mix_w[l]
        pa, pab, pb, pc = (_in_proj(hq, ss, w, 0) for w in (w_a, w_ab, w_b, w_c))
        mix, sa, ca, sb, pbuf = _mixers(pa, pab, pb, pc, state_conv, state_gdn, state_ret, state_pool, l, conv_w[l],
                                        gdn_a_log[l], gdn_dt_bias[l], gdn_norm_w[l], pool_w[l], pool_scale[l],
                                        sched, n_pos)
        x, hq, ss = _proj_residual(mix, w_o, 0, x, ffn2_norm[l], out_rows)
        hmid, w2 = ffn_up_with_casts(2 * l + 1, hq, ss)
        if l + 1 < depth:
            x, hq, ss = _proj_residual(hmid, w2, 0, x, ffn1_norm[l + 1], ffn_rows)
        else:
            x = _proj_residual(hmid, w2, 0, x, None, ffn_rows)
        for k, v in (("conv", ca), ("pool", pbuf), ("gdn", sa), ("ret", sb)):
            outs[k + "_p"].append(v[:bp])
            outs[k + "_s"].append(v[bp:])

    y_prompt = _final_norm(x, final_norm, 0, mp).reshape(bp, lp, d)
    y_sample = _final_norm(x, final_norm, mp, ms).reshape(bs, ls, d)
    st = {k: jnp.stack(v) for k, v in outs.items()}
    return (y_prompt, y_sample, st["conv_p"], st["gdn_p"], st["ret_p"], st["pool_p"], st["conv_s"], st["gdn_s"],
            st["ret_s"], st["pool_s"])
```

```python
import functools
import math
from typing import NamedTuple

import jax
import jax.numpy as jnp
from jax import lax
from jax.experimental import pallas as pl
from jax.experimental.pallas import tpu as pltpu

CHUNK = 64
EPS = 1e-6
ROPE_BASE = 10000.0
POOL_WINDOWS = (2, 4, 8, 16)
PAST_LEN = 4096
LANES = 128
GDN_HEAD_GROUP = 12
FFN_TILE = 512
VMEM_LIMIT_BYTES = 56 * 1024 * 1024

f32 = jnp.float32
bf16 = jnp.bfloat16
HI = lax.Precision.HIGHEST


def _pick(n, candidates):
    for c in candidates:
        if n % c == 0:
            return c
    raise ValueError(f"no block size in {candidates} divides {n}")


def _params(sem):
    return pltpu.CompilerParams(dimension_semantics=sem, vmem_limit_bytes=VMEM_LIMIT_BYTES)


def _silu(x):
    return x * jax.nn.sigmoid(x)


def _dot(a, b):
    return jnp.dot(a, b, preferred_element_type=f32)


def _dot_hi(a, b):
    return jnp.dot(a, b, preferred_element_type=f32, precision=HI)


def _dot_nt(a, b):
    return lax.dot_general(a, b, (((1,), (1,)), ((), ())), preferred_element_type=f32)


def _dot_tn(a, b):
    return lax.dot_general(a, b, (((0,), (0,)), ((), ())), preferred_element_type=f32)


def _row_rsqrt(ss_ref, d_model):
    return lax.rsqrt(ss_ref[:, 0:1] / d_model + EPS)


def _norm_prep_kernel(*refs, na, job_dims):
    n_jobs = len(job_dims)
    xa_ref, xb_ref, g_ref = refs[:3]
    src_refs = refs[3:3 + n_jobs]
    hq_ref, ss_ref = refs[3 + n_jobs:5 + n_jobs]
    dst_refs = refs[5 + n_jobs:]

    def emit(src_ref):
        x = src_ref[...]
        hq_ref[...] = (x * g_ref[...]).astype(hq_ref.dtype)
        ss_ref[...] = jnp.broadcast_to(jnp.sum(x * x, axis=-1, keepdims=True), ss_ref.shape)

    i = pl.program_id(0)
    pl.when(i < na)(lambda: emit(xa_ref))
    pl.when(i >= na)(lambda: emit(xb_ref))
    _run_jobs(job_dims, src_refs, dst_refs, i)


def _norm_prep(xa, xb, g, jobs=()):
    (ma, d), mb = xa.shape, xb.shape[0]
    bm = _pick(math.gcd(ma, mb), (256, 128, 64))
    na, m = ma // bm, ma + mb
    job_in, job_out, job_shape, job_dims = _job_plan(jobs, m // bm, lambda i: i)
    row = pl.BlockSpec((bm, d), lambda i: (i, 0))
    res = pl.pallas_call(
        functools.partial(_norm_prep_kernel, na=na, job_dims=job_dims),
        grid=(m // bm,),
        in_specs=[
            pl.BlockSpec((bm, d), lambda i: (jnp.minimum(i, na - 1), 0)),
            pl.BlockSpec((bm, d), lambda i: (jnp.maximum(i - na, 0), 0)),
            pl.BlockSpec((1, d), lambda i: (0, 0)),
        ] + job_in,
        out_specs=[row, pl.BlockSpec((bm, LANES), lambda i: (i, 0))] + job_out,
        out_shape=[jax.ShapeDtypeStruct((m, d), bf16), jax.ShapeDtypeStruct((m, LANES), f32)] + job_shape,
        compiler_params=_params(("arbitrary",)),
        name="norm_prep",
    )(xa, xb, g.reshape(1, d), *[job.src for job in jobs])
    return res[0], res[1], _split_copies(jobs, res[2:])


def _final_norm_kernel(x_ref, g_ref, o_ref):
    x = x_ref[...]
    o_ref[...] = x * lax.rsqrt(jnp.mean(x * x, axis=-1, keepdims=True) + EPS) * g_ref[...]


def _final_norm(x, g, row0, n_rows):
    d = x.shape[1]
    bm = _pick(n_rows, (512, 256, 128, 64))
    assert row0 % bm == 0
    b0 = row0 // bm
    return pl.pallas_call(
        _final_norm_kernel,
        grid=(n_rows // bm,),
        in_specs=[pl.BlockSpec((bm, d), lambda i: (i + b0, 0)), pl.BlockSpec((1, d), lambda i: (0, 0))],
        out_specs=pl.BlockSpec((bm, d), lambda i: (i, 0)),
        out_shape=jax.ShapeDtypeStruct((n_rows, d), f32),
        compiler_params=_params(("parallel",)),
        name="final_norm",
    )(x, g.reshape(1, d))


class _CastJob(NamedTuple):
    src: jax.Array
    layer: int
    rows_out: int
    windows: tuple


def _job_tile_rows(job, n_steps):
    rows = job.src.shape[1]
    for tr in (16, 32, 64, 128, 256, 512, 1024):
        if rows % tr == 0 and job.rows_out % tr == 0 and job.rows_out // tr <= n_steps:
            return tr
    raise ValueError("cast job does not fit the host grid")


def _job_plan(jobs, n_steps, step_of):
    in_specs, out_specs, out_shape, dims = [], [], [], []
    for job in jobs:
        tr = _job_tile_rows(job, n_steps)
        n_src, n_all = job.src.shape[1] // tr, job.rows_out // tr
        in_specs.append(pl.BlockSpec(
            (None, tr, job.src.shape[2]),
            lambda *g, n_src=n_src, layer=job.layer: (layer, jnp.minimum(step_of(*g), n_src - 1), 0)))
        for _, _, cols_out in job.windows:
            out_specs.append(pl.BlockSpec(
                (tr, cols_out), lambda *g, n_all=n_all: (jnp.minimum(step_of(*g), n_all - 1), 0)))
            out_shape.append(jax.ShapeDtypeStruct((job.rows_out, cols_out), bf16))
        dims.append((tr, job.src.shape[1], job.rows_out, job.windows))
    return in_specs, out_specs, out_shape, tuple(dims)


def _run_jobs(job_dims, src_refs, dst_refs, t):
    dst_refs = list(dst_refs)
    for (tr, rows, rows_out, windows), src_ref in zip(job_dims, src_refs):
        for col0, cols, cols_out in windows:
            dst_ref = dst_refs.pop(0)
            val = src_ref[:, col0:col0 + cols].astype(bf16)
            if rows_out > rows:
                val = jnp.where(t < rows // tr, val, jnp.zeros_like(val))
            dst_ref[:, 0:cols] = val
            if cols_out > cols:
                dst_ref[:, cols:cols_out] = jnp.zeros((tr, cols_out - cols), bf16)


def _split_copies(jobs, outs):
    copies, k = [], 0
    for job in jobs:
        copies.append(list(outs[k:k + len(job.windows)]))
        k += len(job.windows)
    return copies


def _up_kernel(*refs, d_model, scale, job_dims, nj, last_cols):
    n_jobs = len(job_dims)
    h_ref, ss_ref, w1_ref, w3_ref = refs[:4]
    src_refs, o_ref, dst_refs = refs[4:4 + n_jobs], refs[4 + n_jobs], refs[5 + n_jobs:]
    bm, bn = o_ref.shape

    def tile(cols):
        h = h_ref[...]
        r = _row_rsqrt(ss_ref, d_model)
        g = _dot(h, w1_ref[:, 0:cols]) * r
        u = _dot(h, w3_ref[:, 0:cols]) * (r * scale)
        o_ref[:, 0:cols] = (_silu(g) * u).astype(o_ref.dtype)
        if cols < bn:
            o_ref[:, cols:bn] = jnp.zeros((bm, bn - cols), o_ref.dtype)

    if last_cols == bn:
        tile(bn)
    else:
        j = pl.program_id(1)
        pl.when(j != nj // 2)(lambda: tile(bn))
        pl.when(j == nj // 2)(lambda: tile(last_cols))
    _run_jobs(job_dims, src_refs, dst_refs, pl.program_id(0) * nj + pl.program_id(1))


def _short_tile_mid(j, nj):
    return jnp.where(j < nj // 2, j, jnp.where(j == nj // 2, nj - 1, j - 1))


def _wspec(w, layer, row0, rows, bn, col_tile=lambda j: j):
    assert row0 % rows == 0
    if w.ndim == 2:
        return pl.BlockSpec((rows, bn), lambda i, j: (row0 // rows, col_tile(j)))
    return pl.BlockSpec((None, rows, bn), lambda i, j: (layer, row0 // rows, col_tile(j)))


def _ffn_up(hq, ss, w1, w3, layer, f_valid, scale, jobs=()):
    m, d = hq.shape
    f = w1.shape[-1]
    bm = _pick(m, (1024, 512, 256, 128, 64))
    bn = _pick(f, (FFN_TILE, 256, 128))
    ni, nj = m // bm, f // bn
    last_cols = f_valid - (nj - 1) * bn
    assert 0 < last_cols <= bn and last_cols % LANES == 0
    job_in, job_out, job_shape, job_dims = _job_plan(jobs, ni * nj, lambda i, j: i * nj + j)
    col_tile = (lambda j: _short_tile_mid(j, nj)) if last_cols < bn else (lambda j: j)
    in_specs = [
        pl.BlockSpec((bm, d), lambda i, j: (i, 0)),
        pl.BlockSpec((bm, LANES), lambda i, j: (i, 0)),
        _wspec(w1, layer, 0, d, bn, col_tile),
        _wspec(w3, layer, 0, d, bn, col_tile),
    ]
    res = pl.pallas_call(
        functools.partial(_up_kernel, d_model=d, scale=scale, job_dims=job_dims, nj=nj, last_cols=last_cols),
        grid=(ni, nj),
        in_specs=in_specs + job_in,
        out_specs=[pl.BlockSpec((bm, bn), lambda i, j: (i, col_tile(j)))] + job_out,
        out_shape=[jax.ShapeDtypeStruct((m, f), bf16)] + job_shape,
        compiler_params=_params(("arbitrary", "arbitrary")),
        name="ffn_up",
    )(hq, ss, w1, w3, *[job.src for job in jobs])
    return res[0], _split_copies(jobs, res[1:])


def _in_proj_kernel(h_ref, ss_ref, w_ref, o_ref, *, d_model):
    o_ref[...] = _dot(h_ref[...], w_ref[...]) * _row_rsqrt(ss_ref, d_model)


def _in_proj(hq, ss, w, layer):
    m, d = hq.shape
    n = w.shape[-1]
    bm = _pick(m, (1024, 512, 256, 128, 64))
    bn = _pick(n, (1024, 768, 512, 384, 256, 128))
    return pl.pallas_call(
        functools.partial(_in_proj_kernel, d_model=d),
        grid=(m // bm, n // bn),
        in_specs=[
            pl.BlockSpec((bm, d), lambda i, j: (i, 0)),
            pl.BlockSpec((bm, LANES), lambda i, j: (i, 0)),
            _wspec(w, layer, 0, d, bn),
        ],
        out_specs=pl.BlockSpec((bm, bn), lambda i, j: (i, j)),
        out_shape=jax.ShapeDtypeStruct((m, n), f32),
        compiler_params=_params(("parallel", "arbitrary")),
        name="in_proj",
    )(hq, ss, w)


def _proj_residual_kernel(a_ref, w_ref, *rest, with_norm, n_first):
    if n_first is None:
        x, rest = rest[0][...], rest[1:]
    else:
        x, rest = jnp.where(pl.program_id(0) < n_first, rest[0][...], rest[1][...]), rest[2:]
    xn = x + _dot(a_ref[:, 0:w_ref.shape[0]], w_ref[...])
    if not with_norm:
        rest[0][...] = xn
        return
    g_ref, xo_ref, hq_ref, ss_ref = rest
    xo_ref[...] = xn
    hq_ref[...] = (xn * g_ref[...]).astype(hq_ref.dtype)
    part = jnp.broadcast_to(jnp.sum(xn * xn, axis=-1, keepdims=True), ss_ref.shape)
    j = pl.program_id(1)

    @pl.when(j == 0)
    def _():
        ss_ref[...] = part

    @pl.when(j != 0)
    def _():
        ss_ref[...] += part


def _proj_residual(a, w, layer, x, g, bm_cands):
    parts = x if isinstance(x, tuple) else (x,)
    m, d = sum(p.shape[0] for p in parts), parts[0].shape[1]
    bm = _pick(math.gcd(*[p.shape[0] for p in parts]), bm_cands)
    bn = _pick(d, (512, 256, 128))
    assert w.shape[-1] == d and w.shape[-2] <= a.shape[1] and w.shape[-2] % LANES == 0
    with_norm = g is not None
    tile = pl.BlockSpec((bm, bn), lambda i, j: (i, j))
    in_specs = [pl.BlockSpec((bm, a.shape[1]), lambda i, j: (i, 0)), _wspec(w, layer, 0, w.shape[-2], bn)]
    n_first = None
    if len(parts) == 1:
        in_specs.append(tile)
    else:
        n_first = parts[0].shape[0] // bm
        in_specs += [pl.BlockSpec((bm, bn), lambda i, j: (jnp.minimum(i, n_first - 1), j)),
                     pl.BlockSpec((bm, bn), lambda i, j: (jnp.maximum(i - n_first, 0), j))]
    args = [a, w, *parts]
    if with_norm:
        in_specs.append(pl.BlockSpec((1, bn), lambda i, j: (0, j)))
        args.append(g.reshape(1, d))
        out_specs = [tile, tile, pl.BlockSpec((bm, LANES), lambda i, j: (i, 0))]
        out_shape = [jax.ShapeDtypeStruct((m, d), f32), jax.ShapeDtypeStruct((m, d), bf16),
                     jax.ShapeDtypeStruct((m, LANES), f32)]
    else:
        out_specs, out_shape = tile, jax.ShapeDtypeStruct((m, d), f32)
    return pl.pallas_call(
        functools.partial(_proj_residual_kernel, with_norm=with_norm, n_first=n_first),
        grid=(m // bm, d // bn),
        in_specs=in_specs,
        out_specs=out_specs,
        out_shape=out_shape,
        compiler_params=_params(("parallel", "arbitrary")),
        name="proj_residual",
    )(*args)


class _Sched:
    def __init__(self, bp, lp, bs, ls):
        assert lp % CHUNK == 0 and ls % CHUNK == 0 and PAST_LEN % CHUNK == 0
        self.bp, self.np_, self.bs, self.ns = bp, lp // CHUNK, bs, ls // CHUNK
        self.n_prompt = self.bp * self.np_
        self.n_chunks = self.n_prompt + self.bs * self.ns
        self.n_seq = bp + bs

    def split(self, c):
        in_prompt = c < self.n_prompt
        cs = jnp.maximum(c - self.n_prompt, 0)
        seq = jnp.where(in_prompt, c // self.np_, self.bp + cs // self.ns)
        ci = jnp.where(in_prompt, c % self.np_, cs % self.ns)
        last = jnp.where(in_prompt, self.np_ - 1, self.ns - 1)
        return in_prompt, seq, ci, ci == last

    def seq(self, c):
        return self.split(c)[1]

    def sample_seq(self, c):
        return jnp.maximum(self.seq(c) - self.bp, 0)

    def pos_chunk(self, c):
        in_prompt, _, ci, _ = self.split(c)
        return jnp.where(in_prompt, 0, PAST_LEN // CHUNK) + ci


def _strict_lower_inverse_minus_eye(lows, row, col):
    pair = (row // 2) == (col // 2)
    es = [-jnp.where(pair, low, 0.0) for low in lows]
    b = 2
    while b < CHUNK:
        mask = ((row // (2 * b)) == (col // (2 * b))) & ((row // b) != (col // b))
        cs = [jnp.where(mask, low, 0.0) for low in lows]
        xs = [c + _dot(e.astype(bf16), c.astype(bf16)) for e, c in zip(es, cs)]
        ys = [x + _dot(x.astype(bf16), e.astype(bf16)) for x, e in zip(xs, es)]
        es = [e - y for e, y in zip(es, ys)]
        b *= 2
    return es


def _gdn_kernel(qkv_ref, z_ref, ab_ref, cinit_ref, sinit_ref, cw_ref, alog_ref, dtb_ref, nw_ref,
                o_ref, sout_ref, cout_ref, hist_ref, s_ref, gt_ref, *, sched, n_heads, dk, dv, phase):
    c = pl.program_id(0)
    in_prompt, _, ci, is_last = sched.split(c)
    w = n_heads * dk
    hist = hist_ref.shape[0]
    kw = cw_ref.shape[0]

    if phase == "init":
        @pl.when((ci == 0) & in_prompt)
        def _():
            hist_ref[...] = jnp.zeros(hist_ref.shape, f32)
            s_ref[...] = jnp.zeros(s_ref.shape, f32)

        @pl.when((ci == 0) & jnp.logical_not(in_prompt))
        def _():
            hist_ref[0:hist - (kw - 1), :] = jnp.zeros((hist - (kw - 1), hist_ref.shape[1]), f32)
            hist_ref[hist - (kw - 1):hist, :] = cinit_ref[0]
            s_ref[...] = sinit_ref[0]
        return

    if phase == "finish":
        @pl.when(is_last)
        def _():
            sout_ref[0] = s_ref[...]
            cout_ref[0] = qkv_ref[CHUNK - (kw - 1):CHUNK, :]
        return

    ab = ab_ref[...]
    sp = jnp.maximum(ab + dtb_ref[...], 0.0) + jnp.log1p(jnp.exp(-jnp.abs(ab + dtb_ref[...])))
    g_all = -jnp.exp(alog_ref[...]) * sp
    beta_all = jax.nn.sigmoid(ab)

    r2 = lax.broadcasted_iota(jnp.int32, (2 * CHUNK, CHUNK), 0)
    c2 = lax.broadcasted_iota(jnp.int32, (2 * CHUNK, CHUNK), 1)
    gp = _dot_hi((c2 <= r2).astype(f32), g_all)
    gt_ref[...] = gp.T
    g_cum = gp[0:CHUNK, :]
    eg_all = jnp.exp(g_cum)
    g_last = g_cum[CHUNK - 1:CHUNK, :]
    tail_all = jnp.exp(g_last - g_cum)
    gblk_all = jnp.exp(g_last)

    row = lax.broadcasted_iota(jnp.int32, (CHUNK, CHUNK), 0)
    col = lax.broadcasted_iota(jnp.int32, (CHUNK, CHUNK), 1)
    causal = row >= col
    strict = row > col

    def conv(off):
        xt = jnp.concatenate([hist_ref[:, off:off + dk], qkv_ref[:, off:off + dk]], axis=0)
        acc = xt * cw_ref[0:1, off:off + dk]
        for i in range(1, kw):
            acc = pltpu.roll(acc, 1, axis=0) + xt * cw_ref[i:i + 1, off:off + dk]
        return _silu(acc[hist:hist + CHUNK, :])

    def l2n(t, scale=1.0):
        return t * (lax.rsqrt(jnp.sum(t * t, axis=-1, keepdims=True) + EPS) * scale)

    for g0 in range(0, n_heads, GDN_HEAD_GROUP):
        heads = range(g0, min(g0 + GDN_HEAD_GROUP, n_heads))
        lows, rhs_k, ktail_b, qdec_b, qkd_b, bc = [], {}, {}, {}, {}, {}
        for h in heads:
            bc[h] = beta_all[:, n_heads + h:n_heads + h + 1]
            egc = eg_all[:, h:h + 1]
            kn = l2n(conv(w + h * dk))
            kb = kn.astype(bf16)
            kk = _dot_nt(kb, kb)
            qn = l2n(conv(h * dk), dk ** -0.5)
            qk = _dot_nt(qn.astype(bf16), kb)
            decay = jnp.where(causal, jnp.exp(jnp.where(
                causal, g_cum[:, h:h + 1] - gt_ref[h:h + 1, 0:CHUNK], 0.0)), 0.0)
            lows.append(jnp.where(strict, kk * decay * bc[h], 0.0))
            qkd_b[h] = (qk * decay).astype(bf16)
            qdec_b[h] = (qn * egc).astype(bf16)
            ktail_b[h] = (kn * tail_all[:, h:h + 1]).astype(bf16)
            rhs_k[h] = kn * (bc[h] * egc)
        es = dict(zip(heads, _strict_lower_inverse_minus_eye(lows, row, col)))
        rhs = {h: jnp.concatenate([conv(2 * w + h * dv) * bc[h], rhs_k[h]], axis=-1) for h in heads}
        sol = {h: rhs[h] + _dot(es[h].astype(bf16), rhs[h].astype(bf16)) for h in heads}

        sb = {h: s_ref[h].astype(bf16) for h in heads}
        ws = {h: _dot(sol[h][:, dv:dv + dk].astype(bf16), sb[h]) for h in heads}
        ub = {h: (sol[h][:, 0:dv] - ws[h]).astype(bf16) for h in heads}
        qs = {h: _dot(qdec_b[h], sb[h]) for h in heads}
        qu = {h: _dot(qkd_b[h], ub[h]) for h in heads}
        ku = {h: _dot_tn(ktail_b[h], ub[h]) for h in heads}
        for h in heads:
            s_ref[h] = s_ref[h] * gblk_all[:, h:h + 1] + ku[h]
            o = qs[h] + qu[h]
            o = o * lax.rsqrt(jnp.mean(o * o, axis=-1, keepdims=True) + EPS) * nw_ref[...]
            o = o * _silu(z_ref[:, h * dv:(h + 1) * dv])
            o_ref[:, h * dv:(h + 1) * dv] = o.astype(o_ref.dtype)
    hist_ref[...] = qkv_ref[CHUNK - hist:CHUNK, :]


N_GDN_IN, N_RETPOOL_IN = 9, 15


def _mixer_kernel(*refs, sched, gdn_cfg, ret_cfg, gc):
    gdn_in = refs[:N_GDN_IN]
    rp_in = refs[N_GDN_IN:N_GDN_IN + N_RETPOOL_IN]
    mix_ref, sa_ref, cout_ref, sb_ref, bout_ref, xp_ref, s_a, gt_ref, s_b, up_ref = refs[N_GDN_IN + N_RETPOOL_IN:]
    wa = gdn_cfg["n_heads"] * gdn_cfg["dv"]
    wb = ret_cfg["n_heads"] * ret_cfg["dv"]
    wc = mix_ref.shape[1] - wa - wb
    for phase in ("init", "body", "finish"):
        _gdn_kernel(*gdn_in, mix_ref.at[:, pl.ds(0, wa)], sa_ref, cout_ref, xp_ref, s_a, gt_ref, sched=sched,
                    phase=phase, **gdn_cfg)
        _retpool_kernel(*rp_in, mix_ref.at[:, pl.ds(wa, wb)], sb_ref, mix_ref.at[:, pl.ds(wa + wb, wc)], bout_ref,
                        s_b, up_ref, sched=sched, gc=gc, phase=phase, **ret_cfg)


def _mixers(pa, pab, pb, pc, conv_init, gdn_init, ret_init, buf_init, layer, conv_w, a_log, dt_bias, norm_w,
            pool_w, pool_scale, sched, n_pos):
    m, width = pc.shape
    _, _, ha, dka, dva = gdn_init.shape
    _, _, hb, dkb, dvb = ret_init.shape
    n_seq = sched.n_seq
    assert gdn_init.shape[1] == sched.bs
    n_grp, gc, _ = pool_w.shape
    nbuf = buf_init.shape[2]
    kw = conv_w.shape[0]
    assert dka == dva == LANES and 2 * ha <= LANES
    assert dkb == LANES and dvb == 2 * dkb
    assert n_grp == len(POOL_WINDOWS) and nbuf == max(POOL_WINDOWS) - 1 and gc % LANES == 0
    wa, qw, vw = ha * dka, hb * dkb, hb * dvb
    pad = lambda t: jnp.pad(t.reshape(1, -1), ((0, 0), (0, LANES - t.size)))
    cos2, sin2, dmat, qdec, ktail, gblk = _ret_tables(hb, dkb, dvb, n_pos)
    whole = lambda a: pl.BlockSpec(a.shape, lambda c: (0,) * a.ndim)
    seq = sched.seq
    gdn_specs = [
        pl.BlockSpec((CHUNK, 3 * wa), lambda c: (c, 0)),
        pl.BlockSpec((CHUNK, wa), lambda c: (c, 3)),
        pl.BlockSpec((CHUNK, LANES), lambda c: (c, 0)),
        pl.BlockSpec((None, 1, kw - 1, 3 * wa), lambda c: (layer, sched.sample_seq(c), 0, 0)),
        pl.BlockSpec((None, 1, ha, dka, dva), lambda c: (layer, sched.sample_seq(c), 0, 0, 0)),
        pl.BlockSpec((kw, 3 * wa), lambda c: (0, 0)),
        pl.BlockSpec((1, LANES), lambda c: (0, 0)),
        pl.BlockSpec((1, LANES), lambda c: (0, 0)),
        pl.BlockSpec((1, dva), lambda c: (0, 0)),
    ]
    gdn_args = [pa, pa, pab, conv_init, gdn_init, conv_w, pad(a_log), pad(dt_bias), norm_w.reshape(1, dva)]
    rp_specs = [
        pl.BlockSpec((CHUNK, qw), lambda c: (c, 0)),
        pl.BlockSpec((CHUNK, qw), lambda c: (c, 1)),
        pl.BlockSpec((CHUNK, vw), lambda c: (c, 1)),
        pl.BlockSpec((CHUNK, vw), lambda c: (c, 2)),
        pl.BlockSpec((CHUNK, dkb), lambda c: (sched.pos_chunk(c), 0)),
        pl.BlockSpec((CHUNK, dkb), lambda c: (sched.pos_chunk(c), 0)),
        whole(dmat), whole(qdec), whole(ktail), whole(gblk),
        pl.BlockSpec((None, 1, hb, dkb, dvb), lambda c: (layer, sched.sample_seq(c), 0, 0, 0)),
        pl.BlockSpec((CHUNK, width), lambda c: (c, 0)),
        pl.BlockSpec((None, 1, nbuf, width), lambda c: (layer, sched.sample_seq(c), 0, 0)),
        pl.BlockSpec((n_grp, gc, gc), lambda c: (0, 0, 0)),
        pl.BlockSpec((1, width), lambda c: (0, 0)),
    ]
    rp_args = [pb, pb, pb, pb, cos2, sin2, dmat, qdec, ktail, gblk, ret_init, pc, buf_init, pool_w.astype(bf16),
               pool_scale.reshape(1, width)]
    assert len(gdn_specs) == N_GDN_IN and len(rp_specs) == N_RETPOOL_IN
    d_mix = wa + vw + width
    return pl.pallas_call(
        functools.partial(_mixer_kernel, sched=sched, gdn_cfg=dict(n_heads=ha, dk=dka, dv=dva),
                          ret_cfg=dict(n_heads=hb, dk=dkb, dv=dvb), gc=gc),
        grid=(sched.n_chunks,),
        in_specs=gdn_specs + rp_specs,
        out_specs=[
            pl.BlockSpec((CHUNK, d_mix), lambda c: (c, 0)),
            pl.BlockSpec((1, ha, dka, dva), lambda c: (seq(c), 0, 0, 0)),
            pl.BlockSpec((1, kw - 1, 3 * wa), lambda c: (seq(c), 0, 0)),
            pl.BlockSpec((1, hb, dkb, dvb), lambda c: (seq(c), 0, 0, 0)),
            pl.BlockSpec((1, nbuf, width), lambda c: (seq(c), 0, 0)),
        ],
        out_shape=[
            jax.ShapeDtypeStruct((m, d_mix), bf16),
            jax.ShapeDtypeStruct((n_seq, ha, dka, dva), f32),
            jax.ShapeDtypeStruct((n_seq, kw - 1, 3 * wa), f32),
            jax.ShapeDtypeStruct((n_seq, hb, dkb, dvb), f32),
            jax.ShapeDtypeStruct((n_seq, nbuf, width), f32),
        ],
        scratch_shapes=[
            pltpu.VMEM((8, 3 * wa), f32),
            pltpu.VMEM((ha, dka, dva), f32),
            pltpu.VMEM((LANES, 2 * CHUNK), f32),
            pltpu.VMEM((hb, dkb, dvb), f32),
            pltpu.VMEM((16, width), f32),
        ],
        compiler_params=_params(("arbitrary",)),
        name="mixers",
    )(*gdn_args, *rp_args)


def _retpool_kernel(q_ref, k_ref, v_ref, g_ref, cos_ref, sin_ref, dmat_ref, qdec_ref, ktail_ref, gblk_ref, sinit_ref,
                    u_ref, binit_ref, pw_ref, pscale_ref, o_ref, sout_ref, oc_ref, bout_ref, s_ref, hist_ref,
                    *, sched, n_heads, dk, dv, gc, phase):
    c = pl.program_id(0)
    in_prompt, _, ci, is_last = sched.split(c)
    hist = hist_ref.shape[0]
    nbuf = binit_ref.shape[1]

    if phase == "init":
        @pl.when((ci == 0) & in_prompt)
        def _():
            s_ref[...] = jnp.zeros(s_ref.shape, f32)
            hist_ref[...] = jnp.zeros(hist_ref.shape, f32)

        @pl.when((ci == 0) & jnp.logical_not(in_prompt))
        def _():
            s_ref[...] = sinit_ref[0]
            hist_ref[0:hist - nbuf, :] = jnp.zeros((hist - nbuf, hist_ref.shape[1]), f32)
            hist_ref[hist - nbuf:hist, :] = binit_ref[0]
        return

    if phase == "finish":
        @pl.when(is_last)
        def _():
            sout_ref[0] = s_ref[...]
            bout_ref[0] = u_ref[CHUNK - nbuf:CHUNK, :]
        return

    cos = cos_ref[...]
    sin = sin_ref[...]

    def rope(t):
        return t * cos + pltpu.roll(t, dk // 2, axis=1) * sin

    heads = range(n_heads)
    qr = [rope(q_ref[:, h * dk:(h + 1) * dk]) for h in heads]
    kr = [rope(k_ref[:, h * dk:(h + 1) * dk]) * (dk ** -0.5) for h in heads]
    vb = [v_ref[:, h * dv:(h + 1) * dv].astype(bf16) for h in heads]
    scores = [_dot_nt(qr[h].astype(bf16), kr[h].astype(bf16)) * dmat_ref[h] for h in heads]
    cross = [_dot((qr[h] * qdec_ref[h]).astype(bf16), s_ref[h].astype(bf16)) for h in heads]
    inner = [_dot(scores[h].astype(bf16), vb[h]) for h in heads]
    kv = [_dot_tn((kr[h] * ktail_ref[h]).astype(bf16), vb[h]) for h in heads]
    for h in heads:
        s_ref[h] = s_ref[h] * gblk_ref[h] + kv[h]
        o = inner[h] + cross[h]
        oc = o - jnp.mean(o, axis=-1, keepdims=True)
        on = oc * lax.rsqrt(jnp.mean(oc * oc, axis=-1, keepdims=True) + EPS)
        o_ref[:, h * dv:(h + 1) * dv] = (on * _silu(g_ref[:, h * dv:(h + 1) * dv])).astype(o_ref.dtype)

    pos = (jnp.where(in_prompt, 0, PAST_LEN) + ci * CHUNK
           + lax.broadcasted_iota(jnp.int32, (CHUNK, 1), 0))
    for gi, win_len in enumerate(POOL_WINDOWS):
        cols = slice(gi * gc, (gi + 1) * gc)
        ext = jnp.concatenate([hist_ref[:, cols], u_ref[:, cols]], axis=0)
        acc, span = ext, 1
        while span < win_len:
            acc = acc + pltpu.roll(acc, span, axis=0)
            span *= 2
        assert span == win_len and win_len - 1 <= hist
        cur = ext[hist:hist + CHUNK, :]
        cnt = jnp.minimum(pos + 1, win_len).astype(f32)
        d = acc[hist:hist + CHUNK, :] / cnt - cur
        y = _dot(d.astype(bf16), pw_ref[gi]) * pscale_ref[:, cols]
        oc_ref[:, cols] = y.astype(oc_ref.dtype)
    hist_ref[...] = u_ref[CHUNK - hist:CHUNK, :]


def _ret_tables(n_heads, dk, dv, n_pos):
    half = dk // 2
    inv_freq = ROPE_BASE ** (-jnp.arange(half, dtype=f32) / half)
    ang = jnp.arange(n_pos, dtype=jnp.int32).astype(f32)[:, None] * inv_freq[None, :]
    cos, sin = jnp.cos(ang), jnp.sin(ang)
    cos2 = jnp.concatenate([cos, cos], axis=-1)
    sin2 = jnp.concatenate([-sin, sin], axis=-1)
    lg = jnp.log1p(-jnp.exp2(-5.0 - jnp.arange(n_heads, dtype=f32)))
    idx = jnp.arange(CHUNK, dtype=f32)
    rel = idx[:, None] - idx[None, :]
    causal = rel >= 0
    dmat = jnp.where(causal, jnp.exp(jnp.where(causal, rel, 0.0) * lg[:, None, None]), 0.0)
    qdec = jnp.exp((idx + 1.0)[None, :] * lg[:, None])
    ktail = jnp.exp((CHUNK - 1.0 - idx)[None, :] * lg[:, None])
    gblk = jnp.exp(CHUNK * lg)
    bl = lambda t: jnp.broadcast_to(t[:, :, None], (n_heads, CHUNK, dk))
    return cos2, sin2, dmat, bl(qdec), bl(ktail), jnp.broadcast_to(gblk[:, None, None], (n_heads, 1, dv))


def kernel(x_prompt, x_sample, state_conv, state_gdn, state_ret, state_pool, ffn1_norm, ffn1_w1, ffn1_w3, ffn1_w2,
           mix_norm, w_in, conv_w, gdn_a_log, gdn_dt_bias, gdn_norm_w, pool_w, pool_scale, w_out, ffn2_norm,
           ffn2_w1, ffn2_w3, ffn2_w2, final_norm):
    bp, lp, d = x_prompt.shape
    bs, ls, _ = x_sample.shape
    depth = ffn1_w1.shape[0]
    sched = _Sched(bp, lp, bs, ls)
    mp, ms = bp * lp, bs * ls
    n_pos = max(lp, PAST_LEN + ls)

    _, _, ha, dka, dva = state_gdn.shape
    _, _, hb, dkb, dvb = state_ret.shape
    wa, qkb, wb = ha * dva, hb * dkb, hb * dvb
    pool_width = state_pool.shape[-1]
    in_sizes = (2 * ha * dka + wa, wa, ha, ha, qkb, qkb, wb, wb, pool_width)
    assert sum(in_sizes) == w_in.shape[-1]
    offs = [0]
    for s in in_sizes:
        offs.append(offs[-1] + s)

    f_hidden = ffn1_w1.shape[-1]
    f_pad = f_hidden + (-f_hidden % FFN_TILE)
    ffns = []
    for l in range(depth):
        ffns += [(ffn1_w1, ffn1_w3, ffn1_w2, l), (ffn2_w1, ffn2_w3, ffn2_w2, l)]
    up_job = lambda w, layer: _CastJob(w, layer, d, ((0, f_hidden, f_pad),))
    x = (x_prompt.reshape(mp, d), x_sample.reshape(ms, d))
    hq, ss, casts = _norm_prep(*x, ffn1_norm[0], (up_job(ffn1_w1, 0), up_job(ffn1_w3, 0)))
    up_w = [casts[0][0], casts[1][0]]
    span = lambda i, j, out=None: (offs[i], offs[j] - offs[i], out or offs[j] - offs[i])
    in_windows = (span(0, 2), span(2, 4, LANES), span(4, 8), span(8, 9))
    mix_w = {}

    def ffn_up_with_casts(k, hq, ss):
        w2s, layer = ffns[k][2], ffns[k][3]
        jobs = [_CastJob(w2s, layer, f_hidden, ((0, d, d),))]
        if k + 1 < len(ffns):
            n1, n3, _, nl = ffns[k + 1]
            jobs += [up_job(n1, nl), up_job(n3, nl)]
        mix_layer = 0 if k == 0 else (k + 1) // 2 if k % 2 == 1 and (k + 1) // 2 < depth else None
        if mix_layer is not None:
            jobs += [_CastJob(w_in, mix_layer, d, in_windows),
                     _CastJob(w_out, mix_layer, w_out.shape[1], ((0, d, d),))]
        hmid, casts = _ffn_up(hq, ss, up_w[0], up_w[1], 0, f_hidden, 0.5, tuple(jobs))
        if k + 1 < len(ffns):
            up_w[:] = [casts[1][0], casts[2][0]]
        if mix_layer is not None:
            mix_w[mix_layer] = casts[-2] + casts[-1]
        return hmid, casts[0][0]

    ffn_rows = (512, 256, 128, 64)
    out_rows = (1024, 512, 256, 128, 64)

    outs = {k: [] for k in ("conv_p", "gdn_p", "ret_p", "pool_p", "conv_s", "gdn_s", "ret_s", "pool_s")}
    for l in range(depth):
        hmid, w2 = ffn_up_with_casts(2 * l, hq, ss)
        x, hq, ss = _proj_residual(hmid, w2, 0, x, mix_norm[l], ffn_rows)
        w_a, w_ab, w_b, w_c, w_o = mix_w[l]
        pa, pab, pb, pc = (_in_proj(hq, ss, w, 0) for w in (w_a, w_ab, w_b, w_c))
        mix, sa, ca, sb, pbuf = _mixers(pa, pab, pb, pc, state_conv, state_gdn, state_ret, state_pool, l, conv_w[l],
                                        gdn_a_log[l], gdn_dt_bias[l], gdn_norm_w[l], pool_w[l], pool_scale[l],
                                        sched, n_pos)
        x, hq, ss = _proj_residual(mix, w_o, 0, x, ffn2_norm[l], out_rows)
        hmid, w2 = ffn_up_with_casts(2 * l + 1, hq, ss)
        if l + 1 < depth:
            x, hq, ss = _proj_residual(hmid, w2, 0, x, ffn1_norm[l + 1], ffn_rows)
        else:
            x = _proj_residual(hmid, w2, 0, x, None, ffn_rows)
        for k, v in (("conv", ca), ("pool", pbuf), ("gdn", sa), ("ret", sb)):
            outs[k + "_p"].append(v[:bp])
            outs[k + "_s"].append(v[bp:])

    y_prompt = _final_norm(x, final_norm, 0, mp).reshape(bp, lp, d)
    y_sample = _final_norm(x, final_norm, mp, ms).reshape(bs, ls, d)
    st = {k: jnp.stack(v) for k, v in outs.items()}
    return (y_prompt, y_sample, st["conv_p"], st["gdn_p"], st["ret_p"], st["pool_p"], st["conv_s"], st["gdn_s"],
            st["ret_s"], st["pool_s"])
```

```python
import functools
import math
from typing import NamedTuple

import jax
import jax.numpy as jnp
from jax import lax
from jax.experimental import pallas as pl
from jax.experimental.pallas import tpu as pltpu

CHUNK = 64
EPS = 1e-6
ROPE_BASE = 10000.0
POOL_WINDOWS = (2, 4, 8, 16)
PAST_LEN = 4096
LANES = 128
GDN_HEAD_GROUP = 12
FFN_TILE = 512
VMEM_LIMIT_BYTES = 56 * 1024 * 1024

f32 = jnp.float32
bf16 = jnp.bfloat16
HI = lax.Precision.HIGHEST


def _pick(n, candidates):
    for c in candidates:
        if n % c == 0:
            return c
    raise ValueError(f"no block size in {candidates} divides {n}")


def _params(sem):
    return pltpu.CompilerParams(dimension_semantics=sem, vmem_limit_bytes=VMEM_LIMIT_BYTES)


def _silu(x):
    return x * jax.nn.sigmoid(x)


def _dot(a, b):
    return jnp.dot(a, b, preferred_element_type=f32)


def _dot_hi(a, b):
    return jnp.dot(a, b, preferred_element_type=f32, precision=HI)


def _dot_nt(a, b):
    return lax.dot_general(a, b, (((1,), (1,)), ((), ())), preferred_element_type=f32)


def _dot_tn(a, b):
    return lax.dot_general(a, b, (((0,), (0,)), ((), ())), preferred_element_type=f32)


def _row_rsqrt(ss_ref, d_model):
    return lax.rsqrt(ss_ref[:, 0:1] / d_model + EPS)


def _norm_prep_kernel(*refs, na, job_dims):
    n_jobs = len(job_dims)
    xa_ref, xb_ref, g_ref = refs[:3]
    src_refs = refs[3:3 + n_jobs]
    hq_ref, ss_ref = refs[3 + n_jobs:5 + n_jobs]
    dst_refs = refs[5 + n_jobs:]

    def emit(src_ref):
        x = src_ref[...]
        hq_ref[...] = (x * g_ref[...]).astype(hq_ref.dtype)
        ss_ref[...] = jnp.broadcast_to(jnp.sum(x * x, axis=-1, keepdims=True), ss_ref.shape)

    i = pl.program_id(0)
    pl.when(i < na)(lambda: emit(xa_ref))
    pl.when(i >= na)(lambda: emit(xb_ref))
    _run_jobs(job_dims, src_refs, dst_refs, i)


def _norm_prep(xa, xb, g, jobs=()):
    (ma, d), mb = xa.shape, xb.shape[0]
    bm = _pick(math.gcd(ma, mb), (256, 128, 64))
    na, m = ma // bm, ma + mb
    job_in, job_out, job_shape, job_dims = _job_plan(jobs, m // bm, lambda i: i)
    row = pl.BlockSpec((bm, d), lambda i: (i, 0))
    res = pl.pallas_call(
        functools.partial(_norm_prep_kernel, na=na, job_dims=job_dims),
        grid=(m // bm,),
        in_specs=[
            pl.BlockSpec((bm, d), lambda i: (jnp.minimum(i, na - 1), 0)),
            pl.BlockSpec((bm, d), lambda i: (jnp.maximum(i - na, 0), 0)),
            pl.BlockSpec((1, d), lambda i: (0, 0)),
        ] + job_in,
        out_specs=[row, pl.BlockSpec((bm, LANES), lambda i: (i, 0))] + job_out,
        out_shape=[jax.ShapeDtypeStruct((m, d), bf16), jax.ShapeDtypeStruct((m, LANES), f32)] + job_shape,
        compiler_params=_params(("arbitrary",)),
        name="norm_prep",
    )(xa, xb, g.reshape(1, d), *[job.src for job in jobs])
    return res[0], res[1], _split_copies(jobs, res[2:])


def _final_norm_kernel(x_ref, g_ref, o_ref):
    x = x_ref[...]
    o_ref[...] = x * lax.rsqrt(jnp.mean(x * x, axis=-1, keepdims=True) + EPS) * g_ref[...]


def _final_norm(x, g, row0, n_rows):
    d = x.shape[1]
    bm = _pick(n_rows, (512, 256, 128, 64))
    assert row0 % bm == 0
    b0 = row0 // bm
    return pl.pallas_call(
        _final_norm_kernel,
        grid=(n_rows // bm,),
        in_specs=[pl.BlockSpec((bm, d), lambda i: (i + b0, 0)), pl.BlockSpec((1, d), lambda i: (0, 0))],
        out_specs=pl.BlockSpec((bm, d), lambda i: (i, 0)),
        out_shape=jax.ShapeDtypeStruct((n_rows, d), f32),
        compiler_params=_params(("parallel",)),
        name="final_norm",
    )(x, g.reshape(1, d))


class _CastJob(NamedTuple):
    src: jax.Array
    layer: int
    rows_out: int
    windows: tuple


def _job_tile_rows(job, n_steps):
    rows = job.src.shape[1]
    for tr in (16, 32, 64, 128, 256, 512, 1024):
        if rows % tr == 0 and job.rows_out % tr == 0 and job.rows_out // tr <= n_steps:
            return tr
    raise ValueError("cast job does not fit the host grid")


def _job_plan(jobs, n_steps, step_of):
    in_specs, out_specs, out_shape, dims = [], [], [], []
    for job in jobs:
        tr = _job_tile_rows(job, n_steps)
        n_src, n_all = job.src.shape[1] // tr, job.rows_out // tr
        in_specs.append(pl.BlockSpec(
            (None, tr, job.src.shape[2]),
            lambda *g, n_src=n_src, layer=job.layer: (layer, jnp.minimum(step_of(*g), n_src - 1), 0)))
        for _, _, cols_out in job.windows:
            out_specs.append(pl.BlockSpec(
                (tr, cols_out), lambda *g, n_all=n_all: (jnp.minimum(step_of(*g), n_all - 1), 0)))
            out_shape.append(jax.ShapeDtypeStruct((job.rows_out, cols_out), bf16))
        dims.append((tr, job.src.shape[1], job.rows_out, job.windows))
    return in_specs, out_specs, out_shape, tuple(dims)


def _run_jobs(job_dims, src_refs, dst_refs, t):
    dst_refs = list(dst_refs)
    for (tr, rows, rows_out, windows), src_ref in zip(job_dims, src_refs):
        for col0, cols, cols_out in windows:
            dst_ref = dst_refs.pop(0)
            val = src_ref[:, col0:col0 + cols].astype(bf16)
            if rows_out > rows:
                val = jnp.where(t < rows // tr, val, jnp.zeros_like(val))
            dst_ref[:, 0:cols] = val
            if cols_out > cols:
                dst_ref[:, cols:cols_out] = jnp.zeros((tr, cols_out - cols), bf16)


def _split_copies(jobs, outs):
    copies, k = [], 0
    for job in jobs:
        copies.append(list(outs[k:k + len(job.windows)]))
        k += len(job.windows)
    return copies


def _up_kernel(*refs, d_model, scale, job_dims, nj, last_cols):
    n_jobs = len(job_dims)
    h_ref, ss_ref, w1_ref, w3_ref = refs[:4]
    src_refs, o_ref, dst_refs = refs[4:4 + n_jobs], refs[4 + n_jobs], refs[5 + n_jobs:]
    bm, bn = o_ref.shape

    def tile(cols):
        h = h_ref[...]
        r = _row_rsqrt(ss_ref, d_model)
        g = _dot(h, w1_ref[:, 0:cols]) * r
        u = _dot(h, w3_ref[:, 0:cols]) * (r * scale)
        o_ref[:, 0:cols] = (_silu(g) * u).astype(o_ref.dtype)
        if cols < bn:
            o_ref[:, cols:bn] = jnp.zeros((bm, bn - cols), o_ref.dtype)

    if last_cols == bn:
        tile(bn)
    else:
        j = pl.program_id(1)
        pl.when(j != nj // 2)(lambda: tile(bn))
        pl.when(j == nj // 2)(lambda: tile(last_cols))
    _run_jobs(job_dims, src_refs, dst_refs, pl.program_id(0) * nj + pl.program_id(1))


def _short_tile_mid(j, nj):
    return jnp.where(j < nj // 2, j, jnp.where(j == nj // 2, nj - 1, j - 1))


def _wspec(w, layer, row0, rows, bn, col_tile=lambda j: j):
    assert row0 % rows == 0
    if w.ndim == 2:
        return pl.BlockSpec((rows, bn), lambda i, j: (row0 // rows, col_tile(j)))
    return pl.BlockSpec((None, rows, bn), lambda i, j: (layer, row0 // rows, col_tile(j)))


def _ffn_up(hq, ss, w1, w3, layer, f_valid, scale, jobs=()):
    m, d = hq.shape
    f = w1.shape[-1]
    bm = _pick(m, (1024, 512, 256, 128, 64))
    bn = _pick(f, (FFN_TILE, 256, 128))
    ni, nj = m // bm, f // bn
    last_cols = f_valid - (nj - 1) * bn
    assert 0 < last_cols <= bn and last_cols % LANES == 0
    job_in, job_out, job_shape, job_dims = _job_plan(jobs, ni * nj, lambda i, j: i * nj + j)
    col_tile = (lambda j: _short_tile_mid(j, nj)) if last_cols < bn else (lambda j: j)
    in_specs = [
        pl.BlockSpec((bm, d), lambda i, j: (i, 0)),
        pl.BlockSpec((bm, LANES), lambda i, j: (i, 0)),
        _wspec(w1, layer, 0, d, bn, col_tile),
        _wspec(w3, layer, 0, d, bn, col_tile),
    ]
    res = pl.pallas_call(
        functools.partial(_up_kernel, d_model=d, scale=scale, job_dims=job_dims, nj=nj, last_cols=last_cols),
        grid=(ni, nj),
        in_specs=in_specs + job_in,
        out_specs=[pl.BlockSpec((bm, bn), lambda i, j: (i, col_tile(j)))] + job_out,
        out_shape=[jax.ShapeDtypeStruct((m, f), bf16)] + job_shape,
        compiler_params=_params(("arbitrary", "arbitrary")),
        name="ffn_up",
    )(hq, ss, w1, w3, *[job.src for job in jobs])
    return res[0], _split_copies(jobs, res[1:])


def _in_proj_kernel(h_ref, ss_ref, w_ref, o_ref, *, d_model):
    o_ref[...] = _dot(h_ref[...], w_ref[...]) * _row_rsqrt(ss_ref, d_model)


def _in_proj(hq, ss, w, layer):
    m, d = hq.shape
    n = w.shape[-1]
    bm = _pick(m, (1024, 512, 256, 128, 64))
    bn = _pick(n, (1024, 768, 512, 384, 256, 128))
    return pl.pallas_call(
        functools.partial(_in_proj_kernel, d_model=d),
        grid=(m // bm, n // bn),
        in_specs=[
            pl.BlockSpec((bm, d), lambda i, j: (i, 0)),
            pl.BlockSpec((bm, LANES), lambda i, j: (i, 0)),
            _wspec(w, layer, 0, d, bn),
        ],
        out_specs=pl.BlockSpec((bm, bn), lambda i, j: (i, j)),
        out_shape=jax.ShapeDtypeStruct((m, n), f32),
        compiler_params=_params(("parallel", "arbitrary")),
        name="in_proj",
    )(hq, ss, w)


W_RING = 3


def _proj_residual_kernel(a_ref, w_ref, *rest, with_norm, n_first, ring):
    if ring is None:
        w_tile = w_ref[...]
    else:
        (nj, n_steps), (wbuf, sem), rest = ring, rest[-2:], rest[:-2]
        bn = wbuf.shape[2]
        t = pl.program_id(0) * nj + pl.program_id(1)

        def tile_copy(step):
            col = pl.multiple_of((step % nj) * bn, bn)
            slot = step % W_RING
            return pltpu.make_async_copy(w_ref.at[:, pl.ds(col, bn)], wbuf.at[slot], sem.at[slot])

        @pl.when(t == 0)
        def _():
            for s in range(W_RING - 1):
                tile_copy(s).start()

        @pl.when(t + (W_RING - 1) < n_steps)
        def _():
            tile_copy(t + (W_RING - 1)).start()

        tile_copy(t).wait()
        w_tile = wbuf[t % W_RING]
    if n_first is None:
        x, rest = rest[0][...], rest[1:]
    else:
        x, rest = jnp.where(pl.program_id(0) < n_first, rest[0][...], rest[1][...]), rest[2:]
    xn = x + _dot(a_ref[:, 0:w_tile.shape[0]], w_tile)
    if not with_norm:
        rest[0][...] = xn
        return
    g_ref, xo_ref, hq_ref, ss_ref = rest
    xo_ref[...] = xn
    hq_ref[...] = (xn * g_ref[...]).astype(hq_ref.dtype)
    part = jnp.broadcast_to(jnp.sum(xn * xn, axis=-1, keepdims=True), ss_ref.shape)
    j = pl.program_id(1)

    @pl.when(j == 0)
    def _():
        ss_ref[...] = part

    @pl.when(j != 0)
    def _():
        ss_ref[...] += part


def _proj_residual(a, w, layer, x, g, bm_cands, weight_ring=False):
    parts = x if isinstance(x, tuple) else (x,)
    m, d = sum(p.shape[0] for p in parts), parts[0].shape[1]
    bm = _pick(math.gcd(*[p.shape[0] for p in parts]), bm_cands)
    bn = _pick(d, (512, 256, 128))
    assert w.shape[-1] == d and w.shape[-2] <= a.shape[1] and w.shape[-2] % LANES == 0
    with_norm = g is not None
    tile = pl.BlockSpec((bm, bn), lambda i, j: (i, j))
    ni, nj = m // bm, d // bn
    ring, scratch = None, []
    if weight_ring:
        assert w.ndim == 2 and ni * nj >= W_RING
        ring = (nj, ni * nj)
        w_spec = pl.BlockSpec(memory_space=pl.ANY)
        scratch = [pltpu.VMEM((W_RING, w.shape[0], bn), w.dtype), pltpu.SemaphoreType.DMA((W_RING,))]
    else:
        w_spec = _wspec(w, layer, 0, w.shape[-2], bn)
    in_specs = [pl.BlockSpec((bm, a.shape[1]), lambda i, j: (i, 0)), w_spec]
    n_first = None
    if len(parts) == 1:
        in_specs.append(tile)
    else:
        n_first = parts[0].shape[0] // bm
        in_specs += [pl.BlockSpec((bm, bn), lambda i, j: (jnp.minimum(i, n_first - 1), j)),
                     pl.BlockSpec((bm, bn), lambda i, j: (jnp.maximum(i - n_first, 0), j))]
    args = [a, w, *parts]
    if with_norm:
        in_specs.append(pl.BlockSpec((1, bn), lambda i, j: (0, j)))
        args.append(g.reshape(1, d))
        out_specs = [tile, tile, pl.BlockSpec((bm, LANES), lambda i, j: (i, 0))]
        out_shape = [jax.ShapeDtypeStruct((m, d), f32), jax.ShapeDtypeStruct((m, d), bf16),
                     jax.ShapeDtypeStruct((m, LANES), f32)]
    else:
        out_specs, out_shape = tile, jax.ShapeDtypeStruct((m, d), f32)
    return pl.pallas_call(
        functools.partial(_proj_residual_kernel, with_norm=with_norm, n_first=n_first, ring=ring),
        grid=(ni, nj),
        in_specs=in_specs,
        out_specs=out_specs,
        out_shape=out_shape,
        scratch_shapes=scratch,
        compiler_params=_params(("arbitrary" if weight_ring else "parallel", "arbitrary")),
        name="proj_residual",
    )(*args)


class _Sched:
    def __init__(self, bp, lp, bs, ls):
        assert lp % CHUNK == 0 and ls % CHUNK == 0 and PAST_LEN % CHUNK == 0
        self.bp, self.np_, self.bs, self.ns = bp, lp // CHUNK, bs, ls // CHUNK
        self.n_prompt = self.bp * self.np_
        self.n_chunks = self.n_prompt + self.bs * self.ns
        self.n_seq = bp + bs

    def split(self, c):
        in_prompt = c < self.n_prompt
        cs = jnp.maximum(c - self.n_prompt, 0)
        seq = jnp.where(in_prompt, c // self.np_, self.bp + cs // self.ns)
        ci = jnp.where(in_prompt, c % self.np_, cs % self.ns)
        last = jnp.where(in_prompt, self.np_ - 1, self.ns - 1)
        return in_prompt, seq, ci, ci == last

    def seq(self, c):
        return self.split(c)[1]

    def sample_seq(self, c):
        return jnp.maximum(self.seq(c) - self.bp, 0)

    def pos_chunk(self, c):
        in_prompt, _, ci, _ = self.split(c)
        return jnp.where(in_prompt, 0, PAST_LEN // CHUNK) + ci


def _strict_lower_inverse_minus_eye(lows, row, col):
    pair = (row // 2) == (col // 2)
    es = [-jnp.where(pair, low, 0.0) for low in lows]
    b = 2
    while b < CHUNK:
        mask = ((row // (2 * b)) == (col // (2 * b))) & ((row // b) != (col // b))
        cs = [jnp.where(mask, low, 0.0) for low in lows]
        xs = [c + _dot(e.astype(bf16), c.astype(bf16)) for e, c in zip(es, cs)]
        ys = [x + _dot(x.astype(bf16), e.astype(bf16)) for x, e in zip(xs, es)]
        es = [e - y for e, y in zip(es, ys)]
        b *= 2
    return es


def _gdn_kernel(qkv_ref, z_ref, ab_ref, cinit_ref, sinit_ref, cw_ref, alog_ref, dtb_ref, nw_ref,
                o_ref, sout_ref, cout_ref, hist_ref, s_ref, gt_ref, *, sched, n_heads, dk, dv, phase):
    c = pl.program_id(0)
    in_prompt, _, ci, is_last = sched.split(c)
    w = n_heads * dk
    hist = hist_ref.shape[0]
    kw = cw_ref.shape[0]

    if phase == "init":
        @pl.when((ci == 0) & in_prompt)
        def _():
            hist_ref[...] = jnp.zeros(hist_ref.shape, f32)
            s_ref[...] = jnp.zeros(s_ref.shape, f32)

        @pl.when((ci == 0) & jnp.logical_not(in_prompt))
        def _():
            hist_ref[0:hist - (kw - 1), :] = jnp.zeros((hist - (kw - 1), hist_ref.shape[1]), f32)
            hist_ref[hist - (kw - 1):hist, :] = cinit_ref[0]
            s_ref[...] = sinit_ref[0]
        return

    if phase == "finish":
        @pl.when(is_last)
        def _():
            sout_ref[0] = s_ref[...]
            cout_ref[0] = qkv_ref[CHUNK - (kw - 1):CHUNK, :]
        return

    ab = ab_ref[...]
    sp = jnp.maximum(ab + dtb_ref[...], 0.0) + jnp.log1p(jnp.exp(-jnp.abs(ab + dtb_ref[...])))
    g_all = -jnp.exp(alog_ref[...]) * sp
    beta_all = jax.nn.sigmoid(ab)

    r2 = lax.broadcasted_iota(jnp.int32, (2 * CHUNK, CHUNK), 0)
    c2 = lax.broadcasted_iota(jnp.int32, (2 * CHUNK, CHUNK), 1)
    gp = _dot_hi((c2 <= r2).astype(f32), g_all)
    gt_ref[...] = gp.T
    g_cum = gp[0:CHUNK, :]
    eg_all = jnp.exp(g_cum)
    g_last = g_cum[CHUNK - 1:CHUNK, :]
    tail_all = jnp.exp(g_last - g_cum)
    gblk_all = jnp.exp(g_last)

    row = lax.broadcasted_iota(jnp.int32, (CHUNK, CHUNK), 0)
    col = lax.broadcasted_iota(jnp.int32, (CHUNK, CHUNK), 1)
    causal = row >= col
    strict = row > col

    def conv(off):
        xt = jnp.concatenate([hist_ref[:, off:off + dk], qkv_ref[:, off:off + dk]], axis=0)
        acc = xt * cw_ref[0:1, off:off + dk]
        for i in range(1, kw):
            acc = pltpu.roll(acc, 1, axis=0) + xt * cw_ref[i:i + 1, off:off + dk]
        return _silu(acc[hist:hist + CHUNK, :])

    def l2n(t, scale=1.0):
        return t * (lax.rsqrt(jnp.sum(t * t, axis=-1, keepdims=True) + EPS) * scale)

    for g0 in range(0, n_heads, GDN_HEAD_GROUP):
        heads = range(g0, min(g0 + GDN_HEAD_GROUP, n_heads))
        lows, rhs_k, ktail_b, qdec_b, qkd_b, bc = [], {}, {}, {}, {}, {}
        for h in heads:
            bc[h] = beta_all[:, n_heads + h:n_heads + h + 1]
            egc = eg_all[:, h:h + 1]
            kn = l2n(conv(w + h * dk))
            kb = kn.astype(bf16)
            kk = _dot_nt(kb, kb)
            qn = l2n(conv(h * dk), dk ** -0.5)
            qk = _dot_nt(qn.astype(bf16), kb)
            decay = jnp.where(causal, jnp.exp(jnp.where(
                causal, g_cum[:, h:h + 1] - gt_ref[h:h + 1, 0:CHUNK], 0.0)), 0.0)
            lows.append(jnp.where(strict, kk * decay * bc[h], 0.0))
            qkd_b[h] = (qk * decay).astype(bf16)
            qdec_b[h] = (qn * egc).astype(bf16)
            ktail_b[h] = (kn * tail_all[:, h:h + 1]).astype(bf16)
            rhs_k[h] = kn * (bc[h] * egc)
        es = dict(zip(heads, _strict_lower_inverse_minus_eye(lows, row, col)))
        rhs = {h: jnp.concatenate([conv(2 * w + h * dv) * bc[h], rhs_k[h]], axis=-1) for h in heads}
        sol = {h: rhs[h] + _dot(es[h].astype(bf16), rhs[h].astype(bf16)) for h in heads}

        sb = {h: s_ref[h].astype(bf16) for h in heads}
        ws = {h: _dot(sol[h][:, dv:dv + dk].astype(bf16), sb[h]) for h in heads}
        ub = {h: (sol[h][:, 0:dv] - ws[h]).astype(bf16) for h in heads}
        qs = {h: _dot(qdec_b[h], sb[h]) for h in heads}
        qu = {h: _dot(qkd_b[h], ub[h]) for h in heads}
        ku = {h: _dot_tn(ktail_b[h], ub[h]) for h in heads}
        for h in heads:
            s_ref[h] = s_ref[h] * gblk_all[:, h:h + 1] + ku[h]
            o = qs[h] + qu[h]
            o = o * lax.rsqrt(jnp.mean(o * o, axis=-1, keepdims=True) + EPS) * nw_ref[...]
            o = o * _silu(z_ref[:, h * dv:(h + 1) * dv])
            o_ref[:, h * dv:(h + 1) * dv] = o.astype(o_ref.dtype)
    hist_ref[...] = qkv_ref[CHUNK - hist:CHUNK, :]


N_GDN_IN, N_RETPOOL_IN = 9, 15


def _mixer_kernel(*refs, sched, gdn_cfg, ret_cfg, gc):
    gdn_in = refs[:N_GDN_IN]
    rp_in = refs[N_GDN_IN:N_GDN_IN + N_RETPOOL_IN]
    mix_ref, sa_ref, cout_ref, sb_ref, bout_ref, xp_ref, s_a, gt_ref, s_b, up_ref = refs[N_GDN_IN + N_RETPOOL_IN:]
    wa = gdn_cfg["n_heads"] * gdn_cfg["dv"]
    wb = ret_cfg["n_heads"] * ret_cfg["dv"]
    wc = mix_ref.shape[1] - wa - wb
    for phase in ("init", "body", "finish"):
        _gdn_kernel(*gdn_in, mix_ref.at[:, pl.ds(0, wa)], sa_ref, cout_ref, xp_ref, s_a, gt_ref, sched=sched,
                    phase=phase, **gdn_cfg)
        _retpool_kernel(*rp_in, mix_ref.at[:, pl.ds(wa, wb)], sb_ref, mix_ref.at[:, pl.ds(wa + wb, wc)], bout_ref,
                        s_b, up_ref, sched=sched, gc=gc, phase=phase, **ret_cfg)


def _mixers(pa, pab, pb, pc, conv_init, gdn_init, ret_init, buf_init, layer, conv_w, a_log, dt_bias, norm_w,
            pool_w, pool_scale, sched, n_pos):
    m, width = pc.shape
    _, _, ha, dka, dva = gdn_init.shape
    _, _, hb, dkb, dvb = ret_init.shape
    n_seq = sched.n_seq
    assert gdn_init.shape[1] == sched.bs
    n_grp, gc, _ = pool_w.shape
    nbuf = buf_init.shape[2]
    kw = conv_w.shape[0]
    assert dka == dva == LANES and 2 * ha <= LANES
    assert dkb == LANES and dvb == 2 * dkb
    assert n_grp == len(POOL_WINDOWS) and nbuf == max(POOL_WINDOWS) - 1 and gc % LANES == 0
    wa, qw, vw = ha * dka, hb * dkb, hb * dvb
    pad = lambda t: jnp.pad(t.reshape(1, -1), ((0, 0), (0, LANES - t.size)))
    cos2, sin2, dmat, qdec, ktail, gblk = _ret_tables(hb, dkb, dvb, n_pos)
    whole = lambda a: pl.BlockSpec(a.shape, lambda c: (0,) * a.ndim)
    seq = sched.seq
    gdn_specs = [
        pl.BlockSpec((CHUNK, 3 * wa), lambda c: (c, 0)),
        pl.BlockSpec((CHUNK, wa), lambda c: (c, 3)),
        pl.BlockSpec((CHUNK, LANES), lambda c: (c, 0)),
        pl.BlockSpec((None, 1, kw - 1, 3 * wa), lambda c: (layer, sched.sample_seq(c), 0, 0)),
        pl.BlockSpec((None, 1, ha, dka, dva), lambda c: (layer, sched.sample_seq(c), 0, 0, 0)),
        pl.BlockSpec((kw, 3 * wa), lambda c: (0, 0)),
        pl.BlockSpec((1, LANES), lambda c: (0, 0)),
        pl.BlockSpec((1, LANES), lambda c: (0, 0)),
        pl.BlockSpec((1, dva), lambda c: (0, 0)),
    ]
    gdn_args = [pa, pa, pab, conv_init, gdn_init, conv_w, pad(a_log), pad(dt_bias), norm_w.reshape(1, dva)]
    rp_specs = [
        pl.BlockSpec((CHUNK, qw), lambda c: (c, 0)),
        pl.BlockSpec((CHUNK, qw), lambda c: (c, 1)),
        pl.BlockSpec((CHUNK, vw), lambda c: (c, 1)),
        pl.BlockSpec((CHUNK, vw), lambda c: (c, 2)),
        pl.BlockSpec((CHUNK, dkb), lambda c: (sched.pos_chunk(c), 0)),
        pl.BlockSpec((CHUNK, dkb), lambda c: (sched.pos_chunk(c), 0)),
        whole(dmat), whole(qdec), whole(ktail), whole(gblk),
        pl.BlockSpec((None, 1, hb, dkb, dvb), lambda c: (layer, sched.sample_seq(c), 0, 0, 0)),
        pl.BlockSpec((CHUNK, width), lambda c: (c, 0)),
        pl.BlockSpec((None, 1, nbuf, width), lambda c: (layer, sched.sample_seq(c), 0, 0)),
        pl.BlockSpec((n_grp, gc, gc), lambda c: (0, 0, 0)),
        pl.BlockSpec((1, width), lambda c: (0, 0)),
    ]
    rp_args = [pb, pb, pb, pb, cos2, sin2, dmat, qdec, ktail, gblk, ret_init, pc, buf_init, pool_w.astype(bf16),
               pool_scale.reshape(1, width)]
    assert len(gdn_specs) == N_GDN_IN and len(rp_specs) == N_RETPOOL_IN
    d_mix = wa + vw + width
    return pl.pallas_call(
        functools.partial(_mixer_kernel, sched=sched, gdn_cfg=dict(n_heads=ha, dk=dka, dv=dva),
                          ret_cfg=dict(n_heads=hb, dk=dkb, dv=dvb), gc=gc),
        grid=(sched.n_chunks,),
        in_specs=gdn_specs + rp_specs,
        out_specs=[
            pl.BlockSpec((CHUNK, d_mix), lambda c: (c, 0)),
            pl.BlockSpec((1, ha, dka, dva), lambda c: (seq(c), 0, 0, 0)),
            pl.BlockSpec((1, kw - 1, 3 * wa), lambda c: (seq(c), 0, 0)),
            pl.BlockSpec((1, hb, dkb, dvb), lambda c: (seq(c), 0, 0, 0)),
            pl.BlockSpec((1, nbuf, width), lambda c: (seq(c), 0, 0)),
        ],
        out_shape=[
            jax.ShapeDtypeStruct((m, d_mix), bf16),
            jax.ShapeDtypeStruct((n_seq, ha, dka, dva), f32),
            jax.ShapeDtypeStruct((n_seq, kw - 1, 3 * wa), f32),
            jax.ShapeDtypeStruct((n_seq, hb, dkb, dvb), f32),
            jax.ShapeDtypeStruct((n_seq, nbuf, width), f32),
        ],
        scratch_shapes=[
            pltpu.VMEM((8, 3 * wa), f32),
            pltpu.VMEM((ha, dka, dva), f32),
            pltpu.VMEM((LANES, 2 * CHUNK), f32),
            pltpu.VMEM((hb, dkb, dvb), f32),
            pltpu.VMEM((16, width), f32),
        ],
        compiler_params=_params(("arbitrary",)),
        name="mixers",
    )(*gdn_args, *rp_args)


def _retpool_kernel(q_ref, k_ref, v_ref, g_ref, cos_ref, sin_ref, dmat_ref, qdec_ref, ktail_ref, gblk_ref, sinit_ref,
                    u_ref, binit_ref, pw_ref, pscale_ref, o_ref, sout_ref, oc_ref, bout_ref, s_ref, hist_ref,
                    *, sched, n_heads, dk, dv, gc, phase):
    c = pl.program_id(0)
    in_prompt, _, ci, is_last = sched.split(c)
    hist = hist_ref.shape[0]
    nbuf = binit_ref.shape[1]

    if phase == "init":
        @pl.when((ci == 0) & in_prompt)
        def _():
            s_ref[...] = jnp.zeros(s_ref.shape, f32)
            hist_ref[...] = jnp.zeros(hist_ref.shape, f32)

        @pl.when((ci == 0) & jnp.logical_not(in_prompt))
        def _():
            s_ref[...] = sinit_ref[0]
            hist_ref[0:hist - nbuf, :] = jnp.zeros((hist - nbuf, hist_ref.shape[1]), f32)
            hist_ref[hist - nbuf:hist, :] = binit_ref[0]
        return

    if phase == "finish":
        @pl.when(is_last)
        def _():
            sout_ref[0] = s_ref[...]
            bout_ref[0] = u_ref[CHUNK - nbuf:CHUNK, :]
        return

    cos = cos_ref[...]
    sin = sin_ref[...]

    def rope(t):
        return t * cos + pltpu.roll(t, dk // 2, axis=1) * sin

    heads = range(n_heads)
    qr = [rope(q_ref[:, h * dk:(h + 1) * dk]) for h in heads]
    kr = [rope(k_ref[:, h * dk:(h + 1) * dk]) * (dk ** -0.5) for h in heads]
    vb = [v_ref[:, h * dv:(h + 1) * dv].astype(bf16) for h in heads]
    scores = [_dot_nt(qr[h].astype(bf16), kr[h].astype(bf16)) * dmat_ref[h] for h in heads]
    cross = [_dot((qr[h] * qdec_ref[h]).astype(bf16), s_ref[h].astype(bf16)) for h in heads]
    inner = [_dot(scores[h].astype(bf16), vb[h]) for h in heads]
    kv = [_dot_tn((kr[h] * ktail_ref[h]).astype(bf16), vb[h]) for h in heads]
    for h in heads:
        s_ref[h] = s_ref[h] * gblk_ref[h] + kv[h]
        o = inner[h] + cross[h]
        oc = o - jnp.mean(o, axis=-1, keepdims=True)
        on = oc * lax.rsqrt(jnp.mean(oc * oc, axis=-1, keepdims=True) + EPS)
        o_ref[:, h * dv:(h + 1) * dv] = (on * _silu(g_ref[:, h * dv:(h + 1) * dv])).astype(o_ref.dtype)

    pos = (jnp.where(in_prompt, 0, PAST_LEN) + ci * CHUNK
           + lax.broadcasted_iota(jnp.int32, (CHUNK, 1), 0))
    for gi, win_len in enumerate(POOL_WINDOWS):
        cols = slice(gi * gc, (gi + 1) * gc)
        ext = jnp.concatenate([hist_ref[:, cols], u_ref[:, cols]], axis=0)
        acc, span = ext, 1
        while span < win_len:
            acc = acc + pltpu.roll(acc, span, axis=0)
            span *= 2
        assert span == win_len and win_len - 1 <= hist
        cur = ext[hist:hist + CHUNK, :]
        cnt = jnp.minimum(pos + 1, win_len).astype(f32)
        d = acc[hist:hist + CHUNK, :] / cnt - cur
        y = _dot(d.astype(bf16), pw_ref[gi]) * pscale_ref[:, cols]
        oc_ref[:, cols] = y.astype(oc_ref.dtype)
    hist_ref[...] = u_ref[CHUNK - hist:CHUNK, :]


def _ret_tables(n_heads, dk, dv, n_pos):
    half = dk // 2
    inv_freq = ROPE_BASE ** (-jnp.arange(half, dtype=f32) / half)
    ang = jnp.arange(n_pos, dtype=jnp.int32).astype(f32)[:, None] * inv_freq[None, :]
    cos, sin = jnp.cos(ang), jnp.sin(ang)
    cos2 = jnp.concatenate([cos, cos], axis=-1)
    sin2 = jnp.concatenate([-sin, sin], axis=-1)
    lg = jnp.log1p(-jnp.exp2(-5.0 - jnp.arange(n_heads, dtype=f32)))
    idx = jnp.arange(CHUNK, dtype=f32)
    rel = idx[:, None] - idx[None, :]
    causal = rel >= 0
    dmat = jnp.where(causal, jnp.exp(jnp.where(causal, rel, 0.0) * lg[:, None, None]), 0.0)
    qdec = jnp.exp((idx + 1.0)[None, :] * lg[:, None])
    ktail = jnp.exp((CHUNK - 1.0 - idx)[None, :] * lg[:, None])
    gblk = jnp.exp(CHUNK * lg)
    bl = lambda t: jnp.broadcast_to(t[:, :, None], (n_heads, CHUNK, dk))
    return cos2, sin2, dmat, bl(qdec), bl(ktail), jnp.broadcast_to(gblk[:, None, None], (n_heads, 1, dv))


def kernel(x_prompt, x_sample, state_conv, state_gdn, state_ret, state_pool, ffn1_norm, ffn1_w1, ffn1_w3, ffn1_w2,
           mix_norm, w_in, conv_w, gdn_a_log, gdn_dt_bias, gdn_norm_w, pool_w, pool_scale, w_out, ffn2_norm,
           ffn2_w1, ffn2_w3, ffn2_w2, final_norm):
    bp, lp, d = x_prompt.shape
    bs, ls, _ = x_sample.shape
    depth = ffn1_w1.shape[0]
    sched = _Sched(bp, lp, bs, ls)
    mp, ms = bp * lp, bs * ls
    n_pos = max(lp, PAST_LEN + ls)

    _, _, ha, dka, dva = state_gdn.shape
    _, _, hb, dkb, dvb = state_ret.shape
    wa, qkb, wb = ha * dva, hb * dkb, hb * dvb
    pool_width = state_pool.shape[-1]
    in_sizes = (2 * ha * dka + wa, wa, ha, ha, qkb, qkb, wb, wb, pool_width)
    assert sum(in_sizes) == w_in.shape[-1]
    offs = [0]
    for s in in_sizes:
        offs.append(offs[-1] + s)

    f_hidden = ffn1_w1.shape[-1]
    f_pad = f_hidden + (-f_hidden % FFN_TILE)
    ffns = []
    for l in range(depth):
        ffns += [(ffn1_w1, ffn1_w3, ffn1_w2, l), (ffn2_w1, ffn2_w3, ffn2_w2, l)]
    up_job = lambda w, layer: _CastJob(w, layer, d, ((0, f_hidden, f_pad),))
    x = (x_prompt.reshape(mp, d), x_sample.reshape(ms, d))
    hq, ss, casts = _norm_prep(*x, ffn1_norm[0], (up_job(ffn1_w1, 0), up_job(ffn1_w3, 0)))
    up_w = [casts[0][0], casts[1][0]]
    span = lambda i, j, out=None: (offs[i], offs[j] - offs[i], out or offs[j] - offs[i])
    in_windows = (span(0, 2), span(2, 4, LANES), span(4, 8), span(8, 9))
    mix_w = {}

    def ffn_up_with_casts(k, hq, ss):
        w2s, layer = ffns[k][2], ffns[k][3]
        jobs = [_CastJob(w2s, layer, f_hidden, ((0, d, d),))]
        if k + 1 < len(ffns):
            n1, n3, _, nl = ffns[k + 1]
            jobs += [up_job(n1, nl), up_job(n3, nl)]
        mix_layer = 0 if k == 0 else (k + 1) // 2 if k % 2 == 1 and (k + 1) // 2 < depth else None
        if mix_layer is not None:
            jobs += [_CastJob(w_in, mix_layer, d, in_windows),
                     _CastJob(w_out, mix_layer, w_out.shape[1], ((0, d, d),))]
        hmid, casts = _ffn_up(hq, ss, up_w[0], up_w[1], 0, f_hidden, 0.5, tuple(jobs))
        if k + 1 < len(ffns):
            up_w[:] = [casts[1][0], casts[2][0]]
        if mix_layer is not None:
            mix_w[mix_layer] = casts[-2] + casts[-1]
        return hmid, casts[0][0]

    ffn_rows = (512, 256, 128, 64)
    out_rows = (1024, 512, 256, 128, 64)

    outs = {k: [] for k in ("conv_p", "gdn_p", "ret_p", "pool_p", "conv_s", "gdn_s", "ret_s", "pool_s")}
    for l in range(depth):
        hmid, w2 = ffn_up_with_casts(2 * l, hq, ss)
        x, hq, ss = _proj_residual(hmid, w2, 0, x, mix_norm[l], ffn_rows)
        w_a, w_ab, w_b, w_c, w_o = mix_w[l]
        pa, pab, pb, pc = (_in_proj(hq, ss, w, 0) for w in (w_a, w_ab, w_b, w_c))
        mix, sa, ca, sb, pbuf = _mixers(pa, pab, pb, pc, state_conv, state_gdn, state_ret, state_pool, l, conv_w[l],
                                        gdn_a_log[l], gdn_dt_bias[l], gdn_norm_w[l], pool_w[l], pool_scale[l],
                                        sched, n_pos)
        x, hq, ss = _proj_residual(mix, w_o, 0, x, ffn2_norm[l], out_rows, weight_ring=True)
        hmid, w2 = ffn_up_with_casts(2 * l + 1, hq, ss)
        if l + 1 < depth:
            x, hq, ss = _proj_residual(hmid, w2, 0, x, ffn1_norm[l + 1], ffn_rows)
        else:
            x = _proj_residual(hmid, w2, 0, x, None, ffn_rows)
        for k, v in (("conv", ca), ("pool", pbuf), ("gdn", sa), ("ret", sb)):
            outs[k + "_p"].append(v[:bp])
            outs[k + "_s"].append(v[bp:])

    y_prompt = _final_norm(x, final_norm, 0, mp).reshape(bp, lp, d)
    y_sample = _final_norm(x, final_norm, mp, ms).reshape(bs, ls, d)
    st = {k: jnp.stack(v) for k, v in outs.items()}
    return (y_prompt, y_sample, st["conv_p"], st["gdn_p"], st["ret_p"], st["pool_p"], st["conv_s"], st["gdn_s"],
            st["ret_s"], st["pool_s"])
```
